```python
import math
import jax, jax.numpy as jnp
from jax import lax
import numpy as np

D_MODEL = 1024
BATCH = 4
SEQ = 8192
DEPTH = 2

GRID_W = 64
CTX_LEN = 256
N_MIXERS = 2
N_POOL_LAYERS = (DEPTH + N_MIXERS - 1) // N_MIXERS
N_DN_LAYERS = DEPTH // N_MIXERS
POOL_WINDOWS = (2, 4, 8, 16)
N_POOL_GROUPS = 4
POOL_GC = D_MODEL // N_POOL_GROUPS
DN_HEADS = 8
DN_DK = D_MODEL // DN_HEADS
DN_DV = D_MODEL // DN_HEADS
DN_CONV_W = 4
DN_CHUNK = 64
DN_IN = 4 * D_MODEL + 4 * DN_HEADS
N_GROUPS = 4
EXPERTS_PER_GROUP = 8
N_EXPERTS = N_GROUPS * EXPERTS_PER_GROUP
TOP_K_EXPERT = 2
D_EXPERT = 512
EPS = 1e-6

kernel_name = "hybrid_pool_deltanet_hmoe_dit"


def rmsnorm(x, w):
    x32 = x.astype(jnp.float32)
    return x32 * lax.rsqrt(jnp.mean(x32 * x32, axis=-1, keepdims=True) + EPS) * w.astype(jnp.float32)


def l2norm(t):
    return t * lax.rsqrt(jnp.sum(t * t, axis=-1, keepdims=True) + EPS)


def adaln_mod(cond, w, b):
    m = jax.nn.silu(cond.astype(jnp.float32)) @ w + b
    return jnp.split(m, 6, axis=-1)


def box_mean(x, win, axis):
    n = x.shape[axis]
    cs = jnp.cumsum(x.astype(jnp.float32), axis=axis)
    cs = jnp.concatenate([jnp.zeros_like(lax.slice_in_dim(cs, 0, 1, axis=axis)), cs], axis=axis)
    t = np.arange(n)
    lo = np.maximum(t - win // 2, 0)
    hi = np.minimum(t + win // 2, n)
    shape = [1] * x.ndim
    shape[axis] = n
    cnt = jnp.asarray((hi - lo).astype(np.float32)).reshape(shape)
    return (jnp.take(cs, hi, axis=axis) - jnp.take(cs, lo, axis=axis)) / cnt


def pool_mixer(h, w_pool, b_pool, scale, rows):
    B, L, _ = h.shape
    hg = h.astype(jnp.float32).reshape(B, L, N_POOL_GROUPS, POOL_GC)
    diffs = []
    for gi, win in enumerate(POOL_WINDOWS):
        u = hg[:, :, gi, :]
        if rows is not None:
            ug = u.reshape(B, rows, GRID_W, POOL_GC)
            m = box_mean(box_mean(ug, win, 1), win, 2).reshape(B, L, POOL_GC)
        else:
            m = box_mean(u, win, 1)
        diffs.append(m - u)
    d = jnp.stack(diffs, axis=2)
    y = jnp.einsum('blgc,gce->blge', d, w_pool).reshape(B, L, D_MODEL)
    return (y + b_pool) * scale


def short_conv(u, w):
    L = u.shape[1]
    left = DN_CONV_W // 2
    right = DN_CONV_W - 1 - left
    up = jnp.pad(u, ((0, 0), (left, right), (0, 0)))
    out = up[:, 0:L] * w[0]
    for k in range(1, DN_CONV_W):
        out = out + up[:, k:k + L] * w[k]
    return out


def dn_project(h, w_in, w_conv):
    B, L, _ = h.shape
    D = D_MODEL
    p = h @ w_in
    qkv = jax.nn.silu(short_conv(p[..., :3 * D], w_conv))
    z = p[..., 3 * D:4 * D]
    ab = p[..., 4 * D:].astype(jnp.float32).reshape(B, L, 4, DN_HEADS).transpose(2, 0, 3, 1)

    def heads(t, dh):
        return t.reshape(B, L, DN_HEADS, dh).transpose(0, 2, 1, 3).astype(jnp.float32)

    q = l2norm(heads(qkv[..., :D], DN_DK)) * (DN_DK ** -0.5)
    k = l2norm(heads(qkv[..., D:2 * D], DN_DK))
    v = heads(qkv[..., 2 * D:], DN_DV)
    return q, k, v, z, ab


def gated_delta_chunked(q, k, v, beta, g, s0, need_output):
    B, H, L, dk = q.shape
    dv = v.shape[-1]
    C = DN_CHUNK
    n = L // C
    q = q.reshape(B, H, n, C, dk)
    k = k.reshape(B, H, n, C, dk)
    v = v.reshape(B, H, n, C, dv)
    beta = beta.reshape(B, H, n, C)
    gc = jnp.cumsum(g.reshape(B, H, n, C), axis=-1)
    strict = np.tril(np.ones((C, C), dtype=bool), -1)
    diff = gc[..., :, None] - gc[..., None, :]
    kb = k * beta[..., None]
    lmat = jnp.where(strict, jnp.einsum('bhnid,bhnjd->bhnij', kb, k) * jnp.exp(jnp.where(strict, diff, 0.0)), 0.0)
    rhs = jnp.concatenate([v * beta[..., None], kb * jnp.exp(gc)[..., None]], axis=-1)
    sol = lax.linalg.triangular_solve(lmat, rhs, left_side=True, lower=True, unit_diagonal=True)
    u, w = sol[..., :dv], sol[..., dv:]
    g_last = gc[..., -1]
    k_dec = k * jnp.exp(g_last[..., None] - gc)[..., None]
    xs = {'u': u, 'w': w, 'k_dec': k_dec}
    if need_output:
        incl = np.tril(np.ones((C, C), dtype=bool), 0)
        xs['qk'] = jnp.where(incl, jnp.einsum('bhnid,bhnjd->bhnij', q, k) * jnp.exp(jnp.where(incl, diff, 0.0)), 0.0)
        xs['q_dec'] = q * jnp.exp(gc)[..., None]
    xs = {name: jnp.moveaxis(t, 2, 0) for name, t in xs.items()}
    xs['g_last'] = jnp.moveaxis(g_last, 2, 0)

    def step(S, xc):
        v_new = xc['u'] - jnp.einsum('bhck,bhkv->bhcv', xc['w'], S)
        S_next = S * jnp.exp(xc['g_last'])[..., None, None] + jnp.einsum('bhck,bhcv->bhkv', xc['k_dec'], v_new)
        if need_output:
            o = jnp.einsum('bhck,bhkv->bhcv', xc['q_dec'], S) + jnp.einsum('bhij,bhjv->bhiv', xc['qk'], v_new)
            return S_next, o
        return S_next, None

    S_final, o = lax.scan(step, s0, xs)
    if need_output:
        o = jnp.moveaxis(o, 0, 2).reshape(B, H, L, dv)
    return o, S_final


def run_direction(q, k, v, a, b, a_log, dt_bias, s0, reverse, need_output):
    g = -jnp.exp(a_log)[None, :, None] * jax.nn.softplus(a + dt_bias[None, :, None])
    beta = jax.nn.sigmoid(b)
    if reverse:
        q, k, v = jnp.flip(q, 2), jnp.flip(k, 2), jnp.flip(v, 2)
        g, beta = jnp.flip(g, 2), jnp.flip(beta, 2)
    o, S = gated_delta_chunked(q, k, v, beta, g, s0, need_output)
    if reverse and need_output:
        o = jnp.flip(o, 2)
    return o, S


def dn_output(o, z, norm_w, w_out):
    B, H, L, dv = o.shape
    o = o.transpose(0, 2, 1, 3)
    o = o * lax.rsqrt(jnp.mean(o * o, axis=-1, keepdims=True) + EPS) * norm_w
    o = o * jax.nn.silu(z.astype(jnp.float32).reshape(B, L, H, dv))
    return o.reshape(B, L, H * dv) @ w_out


def deltanet_mixer(h, hc, w_in, w_conv, a_log, dt_bias, norm_w, w_out, ctx_out):
    q, k, v, z, ab = dn_project(h, w_in, w_conv)
    qc, kc, vc, zc, abc = dn_project(hc, w_in, w_conv)
    s0 = jnp.zeros((h.shape[0], DN_HEADS, DN_DK, DN_DV), jnp.float32)
    o_lat, o_ctx = None, None
    for d in range(2):
        oc_d, s_ctx = run_direction(qc, kc, vc, abc[2 * d], abc[2 * d + 1], a_log[d], dt_bias[d], s0, d == 1, ctx_out)
        o_d, _ = run_direction(q, k, v, ab[2 * d], ab[2 * d + 1], a_log[d], dt_bias[d], s_ctx, d == 1, True)
        o_lat = o_d if o_lat is None else o_lat + o_d
        if ctx_out:
            o_ctx = oc_d if o_ctx is None else o_ctx + oc_d
    y = dn_output(o_lat, z, norm_w, w_out)
    yc = dn_output(o_ctx, zc, norm_w, w_out) if ctx_out else None
    return y, yc


def hier_moe(h, w_rg, b_rg, w_re, b_re, w_gate, w_up, w_down):
    shp = h.shape
    t = h.reshape(-1, D_MODEL)
    pg = jax.nn.softmax((t @ w_rg + b_rg).astype(jnp.float32), axis=-1)
    pg_top, g_idx = lax.top_k(pg, 1)
    elog = (t @ w_re + b_re).astype(jnp.float32).reshape(-1, N_GROUPS, EXPERTS_PER_GROUP)
    elog_sel = jnp.einsum('tge,tg->te', elog, jax.nn.one_hot(g_idx[:, 0], N_GROUPS, dtype=jnp.float32))
    e_top, e_idx = lax.top_k(elog_sel, TOP_K_EXPERT)
    wts = pg_top * jax.nn.softmax(e_top, axis=-1)
    ids = g_idx * EXPERTS_PER_GROUP + e_idx
    gates = jnp.sum(jax.nn.one_hot(ids, N_EXPERTS, dtype=jnp.float32) * wts[..., None], axis=1)

    def expert(acc, xs):
        wg, wu, wd, gt = xs
        hid = jax.nn.silu(t @ wg) * (t @ wu)
        return acc + gt[:, None] * (hid @ wd), None

    acc, _ = lax.scan(expert, jnp.zeros(t.shape, jnp.float32), (w_gate, w_up, w_down, gates.T))
    return acc.reshape(shp)


def setup_inputs(seed: int = 0) -> dict:
    key = jax.random.key(seed)
    ks = jax.random.split(key, 32)
    D = D_MODEL

    def nrm(k, shape, s):
        return jax.random.normal(k, shape, jnp.float32) * s

    dt = jnp.exp(jax.random.uniform(ks[15], (N_DN_LAYERS, 2, DN_HEADS), jnp.float32,
                                    minval=math.log(1e-3), maxval=math.log(1e-1)))
    return {
        'x': nrm(ks[0], (BATCH, SEQ, D), 1.0),
        'c': nrm(ks[1], (BATCH, D), 1.0),
        'ctx': nrm(ks[2], (BATCH, CTX_LEN, D), 1.0),
        'c_ctx': nrm(ks[3], (D,), 1.0),
        'w_ada': nrm(ks[4], (DEPTH, D, 6 * D), 0.5 * D ** -0.5),
        'b_ada': nrm(ks[5], (DEPTH, 6 * D), 0.02),
        'norm_mix': 1.0 + nrm(ks[6], (DEPTH, D), 0.1),
        'norm_ffn': 1.0 + nrm(ks[7], (DEPTH, D), 0.1),
        'w_pool': nrm(ks[8], (N_POOL_LAYERS, N_POOL_GROUPS, POOL_GC, POOL_GC), POOL_GC ** -0.5),
        'b_pool': nrm(ks[9], (N_POOL_LAYERS, D), 0.02),
        'pool_scale': 1.0 + nrm(ks[10], (N_POOL_LAYERS, D), 0.1),
        'w_dn_in': nrm(ks[11], (N_DN_LAYERS, D, DN_IN), D ** -0.5),
        'w_dn_conv': nrm(ks[12], (N_DN_LAYERS, DN_CONV_W, 3 * D), DN_CONV_W ** -0.5),
        'dn_a_log': jnp.log(jax.random.uniform(ks[13], (N_DN_LAYERS, 2, DN_HEADS), jnp.float32, minval=1.0, maxval=16.0)),
        'dn_dt_bias': dt + jnp.log(-jnp.expm1(-dt)),
        'dn_norm': 1.0 + nrm(ks[14], (N_DN_LAYERS, DN_DV), 0.1),
        'w_dn_out': nrm(ks[16], (N_DN_LAYERS, D, D), D ** -0.5),
        'w_rg': nrm(ks[17], (DEPTH, D, N_GROUPS), D ** -0.5),
        'b_rg': nrm(ks[18], (DEPTH, N_GROUPS), 0.01),
        'w_re': nrm(ks[19], (DEPTH, D, N_EXPERTS), D ** -0.5),
        'b_re': nrm(ks[20], (DEPTH, N_EXPERTS), 0.01),
        'w_e_gate': nrm(ks[21], (DEPTH, N_EXPERTS, D, D_EXPERT), D ** -0.5),
        'w_e_up': nrm(ks[22], (DEPTH, N_EXPERTS, D, D_EXPERT), D ** -0.5),
        'w_e_down': nrm(ks[23], (DEPTH, N_EXPERTS, D_EXPERT, D), D_EXPERT ** -0.5),
        'norm_final': 1.0 + nrm(ks[24], (D,), 0.1),
    }


def reference(x, c, ctx, c_ctx, w_ada, b_ada, norm_mix, norm_ffn, w_pool, b_pool, pool_scale,
              w_dn_in, w_dn_conv, dn_a_log, dn_dt_bias, dn_norm, w_dn_out,
              w_rg, b_rg, w_re, b_re, w_e_gate, w_e_up, w_e_down, norm_final):
    out_dtype = x.dtype
    rows = x.shape[1] // GRID_W
    xs = x.astype(jnp.float32)
    cs = ctx.astype(jnp.float32)
    for i in range(DEPTH):
        last = i == DEPTH - 1
        j = i // N_MIXERS
        is_pool = i % N_MIXERS == 0
        sh_m, sc_m, gt_m, sh_f, sc_f, gt_f = [m[:, None, :] for m in adaln_mod(c, w_ada[i], b_ada[i])]
        csh_m, csc_m, cgt_m, csh_f, csc_f, cgt_f = adaln_mod(c_ctx, w_ada[i], b_ada[i])
        hx = rmsnorm(xs, norm_mix[i]) * (1.0 + sc_m) + sh_m
        if is_pool:
            xs = xs + gt_m * pool_mixer(hx, w_pool[j], b_pool[j], pool_scale[j], rows)
            if not last:
                hc = rmsnorm(cs, norm_mix[i]) * (1.0 + csc_m) + csh_m
                cs = cs + cgt_m * pool_mixer(hc, w_pool[j], b_pool[j], pool_scale[j], None)
        else:
            hc = rmsnorm(cs, norm_mix[i]) * (1.0 + csc_m) + csh_m
            y, yc = deltanet_mixer(hx, hc, w_dn_in[j], w_dn_conv[j], dn_a_log[j], dn_dt_bias[j],
                                   dn_norm[j], w_dn_out[j], not last)
            xs = xs + gt_m * y
            if not last:
                cs = cs + cgt_m * yc
        hx = rmsnorm(xs, norm_ffn[i]) * (1.0 + sc_f) + sh_f
        if last:
            xs = xs + gt_f * hier_moe(hx, w_rg[i], b_rg[i], w_re[i], b_re[i], w_e_gate[i], w_e_up[i], w_e_down[i])
        else:
            hc = rmsnorm(cs, norm_ffn[i]) * (1.0 + csc_f) + csh_f
            n_ctx = hc.shape[1]
            f = hier_moe(jnp.concatenate([hc, hx], axis=1), w_rg[i], b_rg[i], w_re[i], b_re[i],
                         w_e_gate[i], w_e_up[i], w_e_down[i])
            cs = cs + cgt_f * f[:, :n_ctx]
            xs = xs + gt_f * f[:, n_ctx:]
    return rmsnorm(xs, norm_final).astype(out_dtype)
```

```python
import functools

import numpy as np
import jax
import jax.numpy as jnp
from jax import lax
from jax.experimental import pallas as pl
from jax.experimental.pallas import tpu as pltpu

GRID_W = 64
POOL_WINDOWS = (2, 4, 8, 16)
DN_HEADS = 8
DN_CONV_W = 4
DN_CHUNK = 64
N_GROUPS = 4
EXPERTS_PER_GROUP = 8
EPS = 1e-6

LANES = 128
SUBLANES = 8
VMEM_LIMIT_BYTES = 56 * 1024 * 1024

COND_ROWS = 8
TOK_TILE = 512
POOL_TILE = 1024
POOL_HALO = 512
MOE_TILE = 512
BF16 = jnp.bfloat16
F32 = jnp.float32
HIGHEST = lax.Precision.HIGHEST


def _dot(a, b):
    return jnp.dot(a, b, preferred_element_type=F32)


def _dot_hi(a, b):
    return jnp.dot(a, b, precision=HIGHEST, preferred_element_type=F32)


def _split_bf16(a):
    hi = a.astype(BF16)
    lo = (a - hi.astype(F32)).astype(BF16)
    return hi, lo


def _params(*sem):
    return pltpu.CompilerParams(dimension_semantics=sem, vmem_limit_bytes=VMEM_LIMIT_BYTES)


def _norm_mod(x, nw, sc, sh):
    return x * lax.rsqrt(jnp.mean(x * x, axis=-1, keepdims=True) + EPS) * nw * (1.0 + sc) + sh


def _silu(x):
    return x * jax.nn.sigmoid(x)


def _ada_kernel(cond_ref, w_ref, b_ref, o_ref):
    o_ref[...] = _dot_hi(_silu(cond_ref[...]), w_ref[...]) + b_ref[...]


def _ada(cond, w_ada, b_ada):
    depth, d, d6 = w_ada.shape
    tn = d6 // 4
    return pl.pallas_call(
        _ada_kernel,
        out_shape=jax.ShapeDtypeStruct((depth, COND_ROWS, d6), F32),
        grid=(depth, d6 // tn),
        in_specs=[
            pl.BlockSpec((COND_ROWS, d), lambda i, j: (0, 0)),
            pl.BlockSpec((None, d, tn), lambda i, j: (i, 0, j)),
            pl.BlockSpec((None, 1, tn), lambda i, j: (i, 0, j)),
        ],
        out_specs=pl.BlockSpec((None, COND_ROWS, tn), lambda i, j: (i, 0, j)),
        compiler_params=_params("arbitrary", "arbitrary"),
        name="ada",
    )(cond, w_ada, b_ada.reshape(depth, 1, d6))


def _pool_group_out(d, whi_ref, wlo_ref, g):
    dhi, dlo = _split_bf16(d)
    return _dot(dhi, whi_ref[g]) + _dot(dlo, whi_ref[g]) + _dot(dhi, wlo_ref[g])


def _pool_lat_kernel(x_ref, xp_ref, xn_ref, mod_ref, nw_ref, whi_ref, wlo_ref, bp_ref, ps_ref, cm_ref, zero_ref,
                     o_ref, h_ref, *, d, n_rows):
    del zero_ref
    b = pl.program_id(0)
    i = pl.program_id(1)
    nt = pl.num_programs(1)
    gc = d // len(POOL_WINDOWS)
    sh = mod_ref[pl.ds(b, 1), 0:d]
    sc = mod_ref[pl.ds(b, 1), d:2 * d]
    gt = mod_ref[pl.ds(b, 1), 2 * d:3 * d]
    nw = nw_ref[...]
    h_ref[0:POOL_HALO] = jnp.where(i > 0, _norm_mod(xp_ref[...], nw, sc, sh), 0.0)
    h_ref[POOL_HALO:POOL_HALO + POOL_TILE] = _norm_mod(x_ref[...], nw, sc, sh)
    h_ref[POOL_HALO + POOL_TILE:] = jnp.where(i < nt - 1, _norm_mod(xn_ref[...], nw, sc, sh), 0.0)

    tok = lax.broadcasted_iota(jnp.int32, (POOL_TILE, 1), 0)
    r = i * (POOL_TILE // GRID_W) + (tok >> 6)
    c = tok & (GRID_W - 1)
    blk = cm_ref.shape[-1]
    for g, win in enumerate(POOL_WINDOWS):
        half = win // 2
        cols = slice(g * gc, (g + 1) * gc)
        rs = None
        for o in range(-half, half):
            part = h_ref[pl.ds(POOL_HALO + o * GRID_W, POOL_TILE), cols]
            rs = part if rs is None else rs + part
        segs = []
        for s in range(POOL_TILE // blk):
            hi, lo = _split_bf16(rs[s * blk:(s + 1) * blk])
            segs.append(_dot(cm_ref[g], hi) + _dot(cm_ref[g], lo))
        box = jnp.concatenate(segs, axis=0)
        cnt = ((jnp.minimum(r + half, n_rows) - jnp.maximum(r - half, 0))
               * (jnp.minimum(c + half, GRID_W) - jnp.maximum(c - half, 0))).astype(F32)
        dd = box / cnt - h_ref[POOL_HALO:POOL_HALO + POOL_TILE, cols]
        y = _pool_group_out(dd, whi_ref, wlo_ref, g)
        o_ref[:, cols] = x_ref[:, cols] + gt[:, cols] * ((y + bp_ref[:, cols]) * ps_ref[:, cols])


def _pool_ctx_kernel(x_ref, mod_ref, nw_ref, whi_ref, wlo_ref, bp_ref, ps_ref, cm_ref, prev_ref, o_ref, *, d, ctx_row):
    del prev_ref
    gc = d // len(POOL_WINDOWS)
    length = x_ref.shape[0]
    sh = mod_ref[ctx_row:ctx_row + 1, 0:d]
    sc = mod_ref[ctx_row:ctx_row + 1, d:2 * d]
    gt = mod_ref[ctx_row:ctx_row + 1, 2 * d:3 * d]
    x = x_ref[...]
    h = _norm_mod(x, nw_ref[...], sc, sh)
    t = lax.broadcasted_iota(jnp.int32, (length, 1), 0)
    for g, win in enumerate(POOL_WINDOWS):
        half = win // 2
        cols = slice(g * gc, (g + 1) * gc)
        hg = h[:, cols]
        hi, lo = _split_bf16(hg)
        box = _dot(cm_ref[g], hi) + _dot(cm_ref[g], lo)
        cnt = (jnp.minimum(t + half, length) - jnp.maximum(t - half, 0)).astype(F32)
        dd = box / cnt - hg
        y = _pool_group_out(dd, whi_ref, wlo_ref, g)
        o_ref[:, cols] = x[:, cols] + gt[:, cols] * ((y + bp_ref[:, cols]) * ps_ref[:, cols])


def _band_matrices(n, period):
    t = np.arange(n)
    out = []
    for win in POOL_WINDOWS:
        half = win // 2
        diff = t[None, :] - t[:, None]
        same = (t[None, :] // period) == (t[:, None] // period)
        out.append(((diff >= -half) & (diff < half) & same).astype(np.float32))
    return jnp.asarray(np.stack(out), dtype=BF16)


def _pool_layer(x, ctx, mod, nw, w_pool, b_pool, pool_scale):
    bsz, n, d = x.shape
    c_len = ctx.shape[1]
    n_groups = len(POOL_WINDOWS)
    gc = d // n_groups
    t_lat, t_ctx = bsz * n, bsz * c_len
    whi, wlo = _split_bf16(w_pool)
    nw2, bp2, ps2 = nw.reshape(1, d), b_pool.reshape(1, d), pool_scale.reshape(1, d)
    nt = n // POOL_TILE
    nh = n // POOL_HALO
    ratio = POOL_TILE // POOL_HALO
    const2 = lambda b, i: (0, 0)
    const3 = lambda b, i: (0, 0, 0)
    lat = pl.pallas_call(
        functools.partial(_pool_lat_kernel, d=d, n_rows=n // GRID_W),
        out_shape=jax.ShapeDtypeStruct((t_lat + t_ctx, d), F32),
        grid=(bsz, nt),
        in_specs=[
            pl.BlockSpec((None, POOL_TILE, d), lambda b, i: (b, i, 0)),
            pl.BlockSpec((None, POOL_HALO, d), lambda b, i: (b, jnp.maximum(ratio * i - 1, 0), 0)),
            pl.BlockSpec((None, POOL_HALO, d), lambda b, i: (b, jnp.minimum(ratio * i + ratio, nh - 1), 0)),
            pl.BlockSpec(mod.shape, const2),
            pl.BlockSpec((1, d), const2),
            pl.BlockSpec((n_groups, gc, gc), const3),
            pl.BlockSpec((n_groups, gc, gc), const3),
            pl.BlockSpec((1, d), const2),
            pl.BlockSpec((1, d), const2),
            pl.BlockSpec((n_groups, 256, 256), const3),
            pl.BlockSpec(memory_space=pl.ANY),
        ],
        out_specs=pl.BlockSpec((POOL_TILE, d), lambda b, i: (b * nt + i, 0)),
        scratch_shapes=[pltpu.VMEM((POOL_TILE + 2 * POOL_HALO, d), F32)],
        input_output_aliases={10: 0},
        compiler_params=_params("arbitrary", "arbitrary"),
        name="pool_lat",
    )(x, x, x, mod, nw2, whi, wlo, bp2, ps2, _band_matrices(256, GRID_W), jnp.zeros((t_lat + t_ctx, d), F32))
    const1 = lambda b: (0, 0)
    const13 = lambda b: (0, 0, 0)
    return pl.pallas_call(
        functools.partial(_pool_ctx_kernel, d=d, ctx_row=bsz),
        out_shape=jax.ShapeDtypeStruct((t_lat + t_ctx, d), F32),
        grid=(bsz,),
        in_specs=[
            pl.BlockSpec((None, c_len, d), lambda b: (b, 0, 0)),
            pl.BlockSpec(mod.shape, const1),
            pl.BlockSpec((1, d), const1),
            pl.BlockSpec((n_groups, gc, gc), const13),
            pl.BlockSpec((n_groups, gc, gc), const13),
            pl.BlockSpec((1, d), const1),
            pl.BlockSpec((1, d), const1),
            pl.BlockSpec((n_groups, c_len, c_len), const13),
            pl.BlockSpec(memory_space=pl.ANY),
        ],
        out_specs=pl.BlockSpec((c_len, d), lambda b: (t_lat // c_len + b, 0)),
        input_output_aliases={8: 0},
        compiler_params=_params("arbitrary"),
        name="pool_ctx",
    )(ctx, mod, nw2, whi, wlo, bp2, ps2, _band_matrices(c_len, c_len), lat)


def _mod_row(tile, lat_tiles, tiles_per_batch, ctx_row):
    return jnp.where(tile < lat_tiles, tile // tiles_per_batch, ctx_row)


def _route_kernel(x_ref, mod_ref, nw_ref, wr_ref, br_ref, ls_ref, rtok_ref, rt_ref, cnt_ref, carry_ref,
                  *, d, lat_tiles, tiles_per_batch, ctx_row):
    i = pl.program_id(0)
    row = _mod_row(i, lat_tiles, tiles_per_batch, ctx_row)
    sh = mod_ref[pl.ds(row, 1), 3 * d:4 * d]
    sc = mod_ref[pl.ds(row, 1), 4 * d:5 * d]
    h = _norm_mod(x_ref[...], nw_ref[...], sc, sh)
    logits = _dot_hi(h, wr_ref[...]) + br_ref[...]
    lane = lax.broadcasted_iota(jnp.int32, logits.shape, 1).astype(F32)
    neg = jnp.float32(-jnp.inf)
    big = jnp.float32(1e9)
    is_g = lane < N_GROUPS
    lg = jnp.where(is_g, logits, neg)
    gmax = jnp.max(lg, axis=1, keepdims=True)
    g_idx = jnp.min(jnp.where(lg == gmax, lane, big), axis=1, keepdims=True)
    pg_top = 1.0 / jnp.sum(jnp.where(is_g, jnp.exp(lg - gmax), 0.0), axis=1, keepdims=True)
    lo = N_GROUPS + EXPERTS_PER_GROUP * g_idx
    le = jnp.where((lane >= lo) & (lane < lo + EXPERTS_PER_GROUP), logits, neg)
    e1 = jnp.max(le, axis=1, keepdims=True)
    i1 = jnp.min(jnp.where(le == e1, lane, big), axis=1, keepdims=True)
    le2 = jnp.where(lane == i1, neg, le)
    e2 = jnp.max(le2, axis=1, keepdims=True)
    i2 = jnp.min(jnp.where(le2 == e2, lane, big), axis=1, keepdims=True)
    r21 = jnp.exp(e2 - e1)
    w1 = pg_top / (1.0 + r21)
    w2 = pg_top * r21 / (1.0 + r21)
    id1 = i1 - N_GROUPS
    id2 = i2 - N_GROUPS

    @pl.when(i == 0)
    def _():
        carry_ref[...] = jnp.zeros_like(carry_ref)

    oh1 = (lane == id1).astype(F32)
    oh2 = (lane == id2).astype(F32)
    cum1 = _dot(ls_ref[...], oh1.astype(BF16))
    cum2 = _dot(ls_ref[...], oh2.astype(BF16))
    tot1 = jnp.sum(oh1, axis=0, keepdims=True)
    tot2 = jnp.sum(oh2, axis=0, keepdims=True)
    carry = carry_ref[...]
    rank1 = jnp.sum(oh1 * (cum1 + carry), axis=1, keepdims=True)
    rank2 = jnp.sum(oh2 * (cum2 + carry + tot1), axis=1, keepdims=True)
    carry = carry + tot1 + tot2
    carry_ref[...] = carry
    cnt_ref[...] = carry

    fields = (id1, id2, rank1, rank2, w1, w2)
    slab = jnp.zeros_like(logits)
    for k, f in enumerate(fields):
        slab = jnp.where(lane == k, f, slab)
    rtok_ref[...] = slab
    rt_ref[...] = slab.T[0:SUBLANES]


def _dispatch_kernel(pos_ref, x_ref, mod_ref, nw_ref, zero_ref, o_ref, h_ref, sem, *, d, lat_tiles, tiles_per_batch, ctx_row):
    del zero_ref
    i = pl.program_id(0)
    row = _mod_row(i, lat_tiles, tiles_per_batch, ctx_row)
    sh = mod_ref[pl.ds(row, 1), 3 * d:4 * d]
    sc = mod_ref[pl.ds(row, 1), 4 * d:5 * d]
    h_ref[...] = _norm_mod(x_ref[...], nw_ref[...], sc, sh)

    def copy(r, k):
        return pltpu.make_async_copy(h_ref.at[pl.ds(r, 1)], o_ref.at[pl.ds(pos_ref[0, k, r], 1)], sem)

    def issue(r, carry):
        copy(r, 0).start()
        copy(r, 1).start()
        return carry

    def drain(r, carry):
        copy(r, 0).wait()
        copy(r, 1).wait()
        return carry

    lax.fori_loop(0, TOK_TILE, issue, 0)
    lax.fori_loop(0, TOK_TILE, drain, 0)


def _gmm_kernel(te_ref, nu_ref, x_ref, wg_ref, wu_ref, wd_ref, o_ref):
    i = pl.program_id(0)

    @pl.when(i < nu_ref[0])
    def _():
        x = x_ref[...].astype(BF16)
        hid = _silu(_dot(x, wg_ref[...])) * _dot(x, wu_ref[...])
        o_ref[...] = _dot(hid.astype(BF16), wd_ref[...])

    @pl.when(i >= nu_ref[0])
    def _():
        o_ref[...] = jnp.zeros_like(o_ref)


def _combine_kernel(pos_ref, x_ref, rtok_ref, mod_ref, nf_ref, y_ref, o_ref, g1_ref, g2_ref, sem,
                    *, d, lat_tiles, tiles_per_batch, ctx_row, final_norm):
    i = pl.program_id(0)
    row = _mod_row(i, lat_tiles, tiles_per_batch, ctx_row)
    gt = mod_ref[pl.ds(row, 1), 5 * d:6 * d]
    bufs = (g1_ref, g2_ref)

    def copy(r, k):
        return pltpu.make_async_copy(y_ref.at[pl.ds(pos_ref[0, k, r], 1)], bufs[k].at[pl.ds(r, 1)], sem)

    def issue(r, carry):
        copy(r, 0).start()
        copy(r, 1).start()
        return carry

    def drain(r, carry):
        copy(r, 0).wait()
        copy(r, 1).wait()
        return carry

    lax.fori_loop(0, TOK_TILE, issue, 0)
    lax.fori_loop(0, TOK_TILE, drain, 0)
    rt = rtok_ref[...]
    w1 = rt[:, 4:5]
    w2 = rt[:, 5:6]
    out = x_ref[...] + gt * (w1 * g1_ref[...] + w2 * g2_ref[...])
    if final_norm:
        out = out * lax.rsqrt(jnp.mean(out * out, axis=-1, keepdims=True) + EPS) * nf_ref[...]
    o_ref[...] = out


def _moe_layer(xflat, n_tok, mod, nw, w_rg, b_rg, w_re, b_re, wg, wu, wd, *, t_lat, n, ctx_row, norm_final):
    d = xflat.shape[1]
    n_exp = w_re.shape[1]
    tiles = n_tok // TOK_TILE
    lat_tiles = t_lat // TOK_TILE
    tiles_per_batch = n // TOK_TILE
    common = dict(d=d, lat_tiles=lat_tiles, tiles_per_batch=tiles_per_batch, ctx_row=ctx_row)
    nw2 = nw.reshape(1, d)
    wr = jnp.zeros((d, LANES), F32).at[:, :N_GROUPS].set(w_rg).at[:, N_GROUPS:N_GROUPS + n_exp].set(w_re)
    br = jnp.zeros((1, LANES), F32).at[0, :N_GROUPS].set(b_rg).at[0, N_GROUPS:N_GROUPS + n_exp].set(b_re)
    lstrict = jnp.asarray(np.tril(np.ones((TOK_TILE, TOK_TILE), np.float32), -1), dtype=BF16)
    const = lambda i: (0, 0)
    rtok, rt, cnt = pl.pallas_call(
        functools.partial(_route_kernel, **common),
        out_shape=(jax.ShapeDtypeStruct((n_tok, LANES), F32),
                   jax.ShapeDtypeStruct((SUBLANES, n_tok), F32),
                   jax.ShapeDtypeStruct((1, LANES), F32)),
        grid=(tiles,),
        in_specs=[
            pl.BlockSpec((TOK_TILE, d), lambda i: (i, 0)),
            pl.BlockSpec(mod.shape, const),
            pl.BlockSpec((1, d), const),
            pl.BlockSpec((d, LANES), const),
            pl.BlockSpec((1, LANES), const),
            pl.BlockSpec((TOK_TILE, TOK_TILE), const),
        ],
        out_specs=(pl.BlockSpec((TOK_TILE, LANES), lambda i: (i, 0)),
                   pl.BlockSpec((SUBLANES, TOK_TILE), lambda i: (0, i)),
                   pl.BlockSpec((1, LANES), const)),
        scratch_shapes=[pltpu.VMEM((1, LANES), F32)],
        compiler_params=_params("arbitrary"),
        name="moe_route",
    )(xflat, mod, nw2, wr, br, lstrict)

    counts = cnt[0, :n_exp].astype(jnp.int32)
    padded = ((counts + MOE_TILE - 1) // MOE_TILE) * MOE_TILE
    ends = jnp.cumsum(padded)
    offsets = ends - padded
    n_sorted_tiles = (2 * n_tok) // MOE_TILE + n_exp
    n_used = (ends[-1] // MOE_TILE).astype(jnp.int32).reshape(1)
    tile_ids = jnp.arange(n_sorted_tiles, dtype=jnp.int32)
    tile_expert = jnp.sum(tile_ids[:, None] * MOE_TILE >= ends[None, :], axis=1).astype(jnp.int32)
    last_expert = jnp.sum(jnp.maximum(n_used - 1, 0) * MOE_TILE >= ends).astype(jnp.int32)
    tile_expert = jnp.minimum(tile_expert, last_expert)
    ids = rt[0:2].astype(jnp.int32)
    pos = offsets[ids] + rt[2:4].astype(jnp.int32)
    pos3 = pos.reshape(2, tiles, TOK_TILE).transpose(1, 0, 2)

    n_sorted = n_sorted_tiles * MOE_TILE
    pos_spec = pl.BlockSpec((1, 2, TOK_TILE), lambda i: (i, 0, 0), memory_space=pltpu.SMEM)
    xs_sorted = pl.pallas_call(
        functools.partial(_dispatch_kernel, **common),
        out_shape=jax.ShapeDtypeStruct((n_sorted, d), F32),
        grid=(tiles,),
        in_specs=[
            pos_spec,
            pl.BlockSpec((TOK_TILE, d), lambda i: (i, 0)),
            pl.BlockSpec(mod.shape, const),
            pl.BlockSpec((1, d), const),
            pl.BlockSpec(memory_space=pl.ANY),
        ],
        out_specs=pl.BlockSpec(memory_space=pl.ANY),
        scratch_shapes=[pltpu.VMEM((TOK_TILE, d), F32), pltpu.SemaphoreType.DMA(())],
        input_output_aliases={4: 0},
        compiler_params=_params("arbitrary"),
        name="moe_dispatch",
    )(pos3, xflat, mod, nw2, jnp.zeros((n_sorted, d), F32))

    de = wg.shape[-1]
    y_sorted = pl.pallas_call(
        _gmm_kernel,
        out_shape=jax.ShapeDtypeStruct((n_sorted, d), F32),
        grid_spec=pltpu.PrefetchScalarGridSpec(
            num_scalar_prefetch=2,
            grid=(n_sorted_tiles,),
            in_specs=[
                pl.BlockSpec((MOE_TILE, d), lambda i, te, nu: (jnp.minimum(i, jnp.maximum(nu[0] - 1, 0)), 0)),
                pl.BlockSpec((None, d, de), lambda i, te, nu: (te[i], 0, 0)),
                pl.BlockSpec((None, d, de), lambda i, te, nu: (te[i], 0, 0)),
                pl.BlockSpec((None, de, d), lambda i, te, nu: (te[i], 0, 0)),
            ],
            out_specs=pl.BlockSpec((MOE_TILE, d), lambda i, te, nu: (i, 0)),
        ),
        compiler_params=_params("arbitrary"),
        name="moe_gmm",
    )(tile_expert, n_used, xs_sorted, wg.astype(BF16), wu.astype(BF16), wd.astype(BF16))

    final = norm_final is not None
    nf = (norm_final if final else jnp.ones((d,), F32)).reshape(1, d)
    return pl.pallas_call(
        functools.partial(_combine_kernel, final_norm=final, **common),
        out_shape=jax.ShapeDtypeStruct((n_tok, d), F32),
        grid=(tiles,),
        in_specs=[
            pos_spec,
            pl.BlockSpec((TOK_TILE, d), lambda i: (i, 0)),
            pl.BlockSpec((TOK_TILE, LANES), lambda i: (i, 0)),
            pl.BlockSpec(mod.shape, const),
            pl.BlockSpec((1, d), const),
            pl.BlockSpec(memory_space=pl.ANY),
        ],
        out_specs=pl.BlockSpec((TOK_TILE, d), lambda i: (i, 0)),
        scratch_shapes=[pltpu.VMEM((TOK_TILE, d), F32), pltpu.VMEM((TOK_TILE, d), F32), pltpu.SemaphoreType.DMA(())],
        compiler_params=_params("arbitrary"),
        name="moe_combine",
    )(pos3, xflat, rtok, mod, nf, y_sorted)


def _dn_proj_kernel(x_ref, xp_ref, xn_ref, mod_ref, nw_ref, wqkv_ref, wz_ref, wab_ref, wconv_ref, gpar_ref,
                    mpre_ref, msuf_ref, q_ref, k_ref, v_ref, z_ref, gct_ref, gcr_ref, hf_ref,
                    *, d, tile, tiles_per_seq, lat, ctx_row):
    i = pl.program_id(0)
    row = (i // tiles_per_seq) if lat else ctx_row
    sh = mod_ref[pl.ds(row, 1), 0:d]
    sc = mod_ref[pl.ds(row, 1), d:2 * d]
    nw = nw_ref[...]
    first = (i % tiles_per_seq) == 0
    last = (i % tiles_per_seq) == tiles_per_seq - 1
    halo = SUBLANES
    hf_ref[0:halo] = jnp.where(first, 0.0, _norm_mod(xp_ref[...], nw, sc, sh))
    hm = _norm_mod(x_ref[...], nw, sc, sh)
    hf_ref[halo:halo + tile] = hm
    hf_ref[halo + tile:] = jnp.where(last, 0.0, _norm_mod(xn_ref[...], nw, sc, sh))
    hb = hf_ref[...].astype(BF16)

    z_ref[...] = _dot(hm.astype(BF16), wz_ref[...])

    ab = _dot_hi(hm, wab_ref[...])
    a_log, dt_bias, is_g, is_f, is_b = (gpar_ref[j:j + 1, :] for j in range(5))
    xg = ab + dt_bias
    softplus = jnp.maximum(xg, 0.0) + jnp.log(1.0 + jnp.exp(-jnp.abs(xg)))
    slab = jnp.where(is_g > 0.5, -jnp.exp(a_log) * softplus, jax.nn.sigmoid(ab))
    gpre = _dot_hi(mpre_ref[...], slab)
    gsuf = _dot_hi(msuf_ref[...], slab)
    gcs = jnp.where(is_f > 0.5, gpre, jnp.where(is_b > 0.5, gsuf, slab))
    gct_ref[...] = gcs
    gcs_t = gcs.T
    n_ab = 4 * DN_HEADS
    for cc in range(tile // DN_CHUNK):
        gcr_ref[cc] = gcs_t[0:n_ab, cc * DN_CHUNK:(cc + 1) * DN_CHUNK]

    dk = d // DN_HEADS
    cw = 4 * dk
    outs = (q_ref, k_ref, v_ref)
    for cc in range(3 * d // cw):
        cols = slice(cc * cw, (cc + 1) * cw)
        pc = _dot(hb, wqkv_ref[:, cols])
        wc = wconv_ref[:, cols]
        n_rows = tile + 2 * halo
        conv = (pltpu.roll(pc, 2, 0)[halo:halo + tile] * wc[0:1]
                + pltpu.roll(pc, 1, 0)[halo:halo + tile] * wc[1:2]
                + pc[halo:halo + tile] * wc[2:3]
                + pltpu.roll(pc, n_rows - 1, 0)[halo:halo + tile] * wc[3:4])
        act = _silu(conv)
        which = (cc * cw) // d
        for hh in range(cw // dk):
            head = ((cc * cw) % d) // dk + hh
            t = act[:, hh * dk:(hh + 1) * dk]
            if which < 2:
                t = t * lax.rsqrt(jnp.sum(t * t, axis=-1, keepdims=True) + EPS)
            if which == 0:
                t = t * (dk ** -0.5)
            outs[which][head] = t


def _dn_proj(xflat, tok_off, n_seq, seq_len, tile, lat, mod, nw, wqkv, wz, wab, wconv, gpar, ctx_row):
    d = xflat.shape[1]
    dk = d // DN_HEADS
    t_n = n_seq * seq_len
    tiles = t_n // tile
    tiles_per_seq = seq_len // tile
    boff = tok_off // tile
    hoff = tok_off // SUBLANES
    hper = tile // SUBLANES
    hmax = xflat.shape[0] // SUBLANES - 1
    idx = np.arange(tile)
    same = (idx[:, None] // DN_CHUNK) == (idx[None, :] // DN_CHUNK)
    mpre = jnp.asarray((same & (idx[None, :] <= idx[:, None])).astype(np.float32))
    msuf = jnp.asarray((same & (idx[None, :] >= idx[:, None])).astype(np.float32))
    const = lambda i: (0, 0)
    hshape = jax.ShapeDtypeStruct((DN_HEADS, t_n, dk), F32)
    hspec = pl.BlockSpec((DN_HEADS, tile, dk), lambda i: (0, i, 0))
    n_ab = 4 * DN_HEADS
    return pl.pallas_call(
        functools.partial(_dn_proj_kernel, d=d, tile=tile, tiles_per_seq=tiles_per_seq, lat=lat, ctx_row=ctx_row),
        out_shape=(hshape, hshape, hshape,
                   jax.ShapeDtypeStruct((t_n, d), F32),
                   jax.ShapeDtypeStruct((t_n, LANES), F32),
                   jax.ShapeDtypeStruct((t_n // DN_CHUNK, n_ab, DN_CHUNK), F32)),
        grid=(tiles,),
        in_specs=[
            pl.BlockSpec((tile, d), lambda i: (boff + i, 0)),
            pl.BlockSpec((SUBLANES, d), lambda i: (jnp.maximum(hoff + i * hper - 1, 0), 0)),
            pl.BlockSpec((SUBLANES, d), lambda i: (jnp.minimum(hoff + (i + 1) * hper, hmax), 0)),
            pl.BlockSpec(mod.shape, const),
            pl.BlockSpec((1, d), const),
            pl.BlockSpec(wqkv.shape, const),
            pl.BlockSpec(wz.shape, const),
            pl.BlockSpec(wab.shape, const),
            pl.BlockSpec(wconv.shape, const),
            pl.BlockSpec(gpar.shape, const),
            pl.BlockSpec((tile, tile), const),
            pl.BlockSpec((tile, tile), const),
        ],
        out_specs=(hspec, hspec, hspec,
                   pl.BlockSpec((tile, d), lambda i: (i, 0)),
                   pl.BlockSpec((tile, LANES), lambda i: (i, 0)),
                   pl.BlockSpec((tile // DN_CHUNK, n_ab, DN_CHUNK), lambda i: (i, 0, 0))),
        scratch_shapes=[pltpu.VMEM((tile + 2 * SUBLANES, d), F32)],
        compiler_params=_params("arbitrary"),
        name="dn_proj_lat" if lat else "dn_proj_ctx",
    )(xflat, xflat, xflat, mod, nw.reshape(1, d), wqkv, wz, wab, wconv, gpar, mpre, msuf)


def _tri_solve(a, rhs):
    p = -a
    x = rhs + _dot(p.astype(BF16), rhs.astype(BF16))
    for _ in range(5):
        pb = p.astype(BF16)
        p = _dot(pb, pb)
        x = x + _dot(p.astype(BF16), x.astype(BF16))
    return x


def _dn_chunk_kernel(q_ref, k_ref, v_ref, gct_ref, gcr_ref, s0_ref, *rest, blk, reverse, need_output):
    if need_output:
        o_ref, sf_ref, s_ref = rest
    else:
        sf_ref, s_ref = rest
        o_ref = None
    j = pl.program_id(1)
    nchunk = blk // DN_CHUNK
    c = DN_CHUNK

    @pl.when(j == 0)
    def _():
        s_ref[...] = s0_ref[...]

    ii = lax.broadcasted_iota(jnp.int32, (c, c), 0)
    jj = lax.broadcasted_iota(jnp.int32, (c, c), 1)
    strict = (ii < jj) if reverse else (ii > jj)
    incl = (ii <= jj) if reverse else (ii >= jj)
    field = 2 if reverse else 0

    def body(step, carry):
        ci = (nchunk - 1 - step) if reverse else step
        r0 = pl.multiple_of(ci * c, c)
        rows = pl.ds(r0, c)
        gcols = gct_ref[rows, :]
        grows = gcr_ref[ci]
        for h in range(DN_HEADS):
            ln = 4 * h + field
            gcc = gcols[:, ln:ln + 1]
            beta = gcols[:, ln + 1:ln + 2]
            gcr = grows[ln:ln + 1, :]
            glast = gcc[0:1, :] if reverse else gcc[c - 1:c, :]
            qc = q_ref[h, rows, :]
            kc = k_ref[h, rows, :]
            vc = v_ref[h, rows, :]
            diff = gcc - gcr
            decay = jnp.exp(jnp.where(incl, diff, 0.0))
            kb = kc * beta
            kcb = kc.astype(BF16)
            kq = lax.dot_general(jnp.concatenate([kb, qc], axis=0).astype(BF16), kcb,
                                 (((1,), (1,)), ((), ())), preferred_element_type=F32)
            a = jnp.where(strict, kq[0:c] * decay, 0.0)
            rhs = jnp.concatenate([vc * beta, kb * jnp.exp(gcc)], axis=1)
            sol = _tri_solve(a, rhs)
            u = sol[:, 0:vc.shape[1]]
            w = sol[:, vc.shape[1]:]
            s = s_ref[h]
            sb = s.astype(BF16)
            k_dec = kc * jnp.exp(glast - gcc)
            if need_output:
                q_dec = qc * jnp.exp(gcc)
                ws = _dot(jnp.concatenate([w, q_dec], axis=0).astype(BF16), sb)
                v_new = u - ws[0:c]
                qk = jnp.where(incl, kq[c:2 * c] * decay, 0.0)
                o_ref[h, rows, :] = ws[c:2 * c] + _dot(qk.astype(BF16), v_new.astype(BF16))
            else:
                v_new = u - _dot(w.astype(BF16), sb)
            s_ref[h] = s * jnp.exp(glast) + _dot(k_dec.T.astype(BF16), v_new.astype(BF16))
        return carry

    lax.fori_loop(0, nchunk, body, 0)

    @pl.when(j == pl.num_programs(1) - 1)
    def _():
        sf_ref[...] = s_ref[...]


def _dn_scan(q, k, v, gct, gcr, s0, n_seq, seq_len, blk, reverse, need_output):
    dk = q.shape[-1]
    nblk = seq_len // blk
    order = (lambda j: nblk - 1 - j) if reverse else (lambda j: j)
    hspec = pl.BlockSpec((DN_HEADS, blk, dk), lambda b, j: (0, b * nblk + order(j), 0))
    sspec = pl.BlockSpec((None, DN_HEADS, dk, dk), lambda b, j: (b, 0, 0, 0))
    n_ab = gcr.shape[1]
    s_shape = jax.ShapeDtypeStruct((n_seq, DN_HEADS, dk, dk), F32)
    out_shape = [s_shape]
    out_specs = [sspec]
    if need_output:
        out_shape.insert(0, jax.ShapeDtypeStruct(q.shape, F32))
        out_specs.insert(0, hspec)
    res = pl.pallas_call(
        functools.partial(_dn_chunk_kernel, blk=blk, reverse=reverse, need_output=need_output),
        out_shape=tuple(out_shape),
        grid=(n_seq, nblk),
        in_specs=[
            hspec, hspec, hspec,
            pl.BlockSpec((blk, LANES), lambda b, j: (b * nblk + order(j), 0)),
            pl.BlockSpec((blk // DN_CHUNK, n_ab, DN_CHUNK), lambda b, j: (b * nblk + order(j), 0, 0)),
            sspec,
        ],
        out_specs=tuple(out_specs),
        scratch_shapes=[pltpu.VMEM((DN_HEADS, dk, dk), F32)],
        compiler_params=_params("arbitrary", "arbitrary"),
        name="dn_scan_" + ("bwd" if reverse else "fwd") + ("_out" if need_output else "_state"),
    )(q, k, v, gct, gcr, s0)
    return (res[0], res[1]) if need_output else (None, res[0])


def _dn_out_kernel(of_ref, ob_ref, z_ref, x_ref, mod_ref, nrm_ref, wout_ref, o_ref, *, d, tiles_per_batch):
    i = pl.program_id(0)
    row = i // tiles_per_batch
    gt = mod_ref[pl.ds(row, 1), 2 * d:3 * d]
    dk = d // DN_HEADS
    parts = []
    for h in range(DN_HEADS):
        o = of_ref[h] + ob_ref[h]
        o = o * lax.rsqrt(jnp.mean(o * o, axis=-1, keepdims=True) + EPS) * nrm_ref[...]
        parts.append((o * _silu(z_ref[:, h * dk:(h + 1) * dk])).astype(BF16))
    y = _dot(jnp.concatenate(parts, axis=1), wout_ref[...])
    o_ref[...] = x_ref[...] + gt * y


def _deltanet_layer(xflat, bsz, n, c_len, mod, nw, w_in, w_conv, a_log, dt_bias, dn_norm, w_out):
    d = xflat.shape[1]
    dk = d // DN_HEADS
    t_lat = bsz * n
    ctx_row = bsz
    wqkv = w_in[:, :3 * d].astype(BF16)
    wz = w_in[:, 3 * d:4 * d].astype(BF16)
    n_ab = 4 * DN_HEADS
    wab = w_in[:, 4 * d:].reshape(d, 4, DN_HEADS).transpose(0, 2, 1).reshape(d, n_ab)
    wab = jnp.zeros((d, LANES), F32).at[:, :n_ab].set(wab)
    lane = np.arange(LANES)
    fld, head, used = lane % 4, np.minimum(lane // 4, DN_HEADS - 1), lane < n_ab
    is_g = used & (fld % 2 == 0)
    dirn = fld // 2
    gpar = jnp.zeros((SUBLANES, LANES), F32)
    gpar = gpar.at[0].set(jnp.where(is_g, a_log[dirn, head], 0.0))
    gpar = gpar.at[1].set(jnp.where(is_g, dt_bias[dirn, head], 0.0))
    gpar = gpar.at[2].set(jnp.asarray(is_g, F32))
    gpar = gpar.at[3].set(jnp.asarray(used & (fld == 0), F32))
    gpar = gpar.at[4].set(jnp.asarray(used & (fld == 2), F32))

    proj = functools.partial(_dn_proj, xflat, mod=mod, nw=nw, wqkv=wqkv, wz=wz, wab=wab, wconv=w_conv, gpar=gpar,
                             ctx_row=ctx_row)
    q, k, v, z, gct, gcr = proj(tok_off=0, n_seq=bsz, seq_len=n, tile=TOK_TILE, lat=True)
    qc, kc, vc, _, gctc, gcrc = proj(tok_off=t_lat, n_seq=bsz, seq_len=c_len, tile=c_len, lat=False)

    s0 = jnp.zeros((bsz, DN_HEADS, dk, dk), F32)
    o_dir = []
    for reverse in (False, True):
        _, s_ctx = _dn_scan(qc, kc, vc, gctc, gcrc, s0, bsz, c_len, c_len, reverse, False)
        o, _ = _dn_scan(q, k, v, gct, gcr, s_ctx, bsz, n, TOK_TILE, reverse, True)
        o_dir.append(o)

    tiles = t_lat // TOK_TILE
    const = lambda i: (0, 0)
    hspec = pl.BlockSpec((DN_HEADS, TOK_TILE, dk), lambda i: (0, i, 0))
    return pl.pallas_call(
        functools.partial(_dn_out_kernel, d=d, tiles_per_batch=n // TOK_TILE),
        out_shape=jax.ShapeDtypeStruct((t_lat, d), F32),
        grid=(tiles,),
        in_specs=[
            hspec, hspec,
            pl.BlockSpec((TOK_TILE, d), lambda i: (i, 0)),
            pl.BlockSpec((TOK_TILE, d), lambda i: (i, 0)),
            pl.BlockSpec(mod.shape, const),
            pl.BlockSpec((1, dk), const),
            pl.BlockSpec((d, d), const),
        ],
        out_specs=pl.BlockSpec((TOK_TILE, d), lambda i: (i, 0)),
        compiler_params=_params("arbitrary"),
        name="dn_out",
    )(o_dir[0], o_dir[1], z, xflat, mod, dn_norm.reshape(1, dk), w_out.astype(BF16))


def kernel(x, c, ctx, c_ctx, w_ada, b_ada, norm_mix, norm_ffn, w_pool, b_pool, pool_scale, w_dn_in, w_dn_conv, dn_a_log, dn_dt_bias, dn_norm, w_dn_out, w_rg, b_rg, w_re, b_re, w_e_gate, w_e_up, w_e_down, norm_final):
    bsz, n, d = x.shape
    c_len = ctx.shape[1]
    assert w_ada.shape[0] == 2 and bsz < COND_ROWS and n % POOL_TILE == 0 and c_len == 256
    assert (bsz * c_len) % TOK_TILE == 0 and d % (4 * LANES) == 0
    t_lat = bsz * n
    t_all = t_lat + bsz * c_len
    ctx_row = bsz
    cond = jnp.zeros((COND_ROWS, d), F32).at[:bsz].set(c.astype(F32)).at[ctx_row].set(c_ctx.astype(F32))
    mod = _ada(cond, w_ada, b_ada)

    xflat = _pool_layer(x.astype(F32), ctx.astype(F32), mod[0], norm_mix[0], w_pool[0], b_pool[0], pool_scale[0])
    xflat = _moe_layer(xflat, t_all, mod[0], norm_ffn[0], w_rg[0], b_rg[0], w_re[0], b_re[0],
                       w_e_gate[0], w_e_up[0], w_e_down[0], t_lat=t_lat, n=n, ctx_row=ctx_row, norm_final=None)

    xlat = _deltanet_layer(xflat, bsz, n, c_len, mod[1], norm_mix[1], w_dn_in[0], w_dn_conv[0], dn_a_log[0],
                           dn_dt_bias[0], dn_norm[0], w_dn_out[0])
    out = _moe_layer(xlat, t_lat, mod[1], norm_ffn[1], w_rg[1], b_rg[1], w_re[1], b_re[1],
                     w_e_gate[1], w_e_up[1], w_e_down[1], t_lat=t_lat, n=n, ctx_row=ctx_row, norm_final=norm_final)
    return out.reshape(bsz, n, d).astype(x.dtype)
```

```python
import functools

import numpy as np
import jax
import jax.numpy as jnp
from jax import lax
from jax.experimental import pallas as pl
from jax.experimental.pallas import tpu as pltpu

GRID_W = 64
POOL_WINDOWS = (2, 4, 8, 16)
DN_HEADS = 8
DN_CONV_W = 4
DN_CHUNK = 64
DN_PREP_CHUNKS = 2
N_GROUPS = 4
EXPERTS_PER_GROUP = 8
EPS = 1e-6

LANES = 128
SUBLANES = 8
VMEM_LIMIT_BYTES = 56 * 1024 * 1024

COND_ROWS = 8
TOK_TILE = 512
POOL_TILE = 1024
POOL_HALO = 512
MOE_TILE = 512
DMA_UNROLL = 8
BF16 = jnp.bfloat16
F32 = jnp.float32
HIGHEST = lax.Precision.HIGHEST


def _dot(a, b):
    return jnp.dot(a, b, preferred_element_type=F32)


def _dot_hi(a, b):
    return jnp.dot(a, b, precision=HIGHEST, preferred_element_type=F32)


def _split_bf16(a):
    hi = a.astype(BF16)
    lo = (a - hi.astype(F32)).astype(BF16)
    return hi, lo


def _params(*sem):
    return pltpu.CompilerParams(dimension_semantics=sem, vmem_limit_bytes=VMEM_LIMIT_BYTES)


def _norm_mod(x, nw, sc, sh):
    return x * lax.rsqrt(jnp.mean(x * x, axis=-1, keepdims=True) + EPS) * nw * (1.0 + sc) + sh


def _silu(x):
    return x * jax.nn.sigmoid(x)


def _ada_kernel(cond_ref, w_ref, b_ref, o_ref):
    o_ref[...] = _dot_hi(_silu(cond_ref[...]), w_ref[...]) + b_ref[...]


def _ada(cond, w_ada, b_ada):
    depth, d, d6 = w_ada.shape
    tn = d6 // 4
    return pl.pallas_call(
        _ada_kernel,
        out_shape=jax.ShapeDtypeStruct((depth, COND_ROWS, d6), F32),
        grid=(depth, d6 // tn),
        in_specs=[
            pl.BlockSpec((COND_ROWS, d), lambda i, j: (0, 0)),
            pl.BlockSpec((None, d, tn), lambda i, j: (i, 0, j)),
            pl.BlockSpec((None, 1, tn), lambda i, j: (i, 0, j)),
        ],
        out_specs=pl.BlockSpec((None, COND_ROWS, tn), lambda i, j: (i, 0, j)),
        compiler_params=_params("arbitrary", "arbitrary"),
        name="ada",
    )(cond, w_ada, b_ada.reshape(depth, 1, d6))


def _pool_group_out(d, whi_ref, wlo_ref, g):
    dhi, dlo = _split_bf16(d)
    return _dot(dhi, whi_ref[g]) + _dot(dlo, whi_ref[g]) + _dot(dhi, wlo_ref[g])


def _pool_lat_kernel(x_ref, xp_ref, xn_ref, mod_ref, nw_ref, whi_ref, wlo_ref, bp_ref, ps_ref, cm_ref, zero_ref,
                     o_ref, h_ref, *, d, n_rows):
    del zero_ref
    b = pl.program_id(0)
    i = pl.program_id(1)
    nt = pl.num_programs(1)
    gc = d // len(POOL_WINDOWS)
    sh = mod_ref[pl.ds(b, 1), 0:d]
    sc = mod_ref[pl.ds(b, 1), d:2 * d]
    gt = mod_ref[pl.ds(b, 1), 2 * d:3 * d]
    nw = nw_ref[...]
    h_ref[0:POOL_HALO] = jnp.where(i > 0, _norm_mod(xp_ref[...], nw, sc, sh), 0.0)
    h_ref[POOL_HALO:POOL_HALO + POOL_TILE] = _norm_mod(x_ref[...], nw, sc, sh)
    h_ref[POOL_HALO + POOL_TILE:] = jnp.where(i < nt - 1, _norm_mod(xn_ref[...], nw, sc, sh), 0.0)

    tok = lax.broadcasted_iota(jnp.int32, (POOL_TILE, 1), 0)
    r = i * (POOL_TILE // GRID_W) + (tok >> 6)
    c = tok & (GRID_W - 1)
    blk = cm_ref.shape[-1]
    for g, win in enumerate(POOL_WINDOWS):
        half = win // 2
        cols = slice(g * gc, (g + 1) * gc)
        rs = None
        for o in range(-half, half):
            part = h_ref[pl.ds(POOL_HALO + o * GRID_W, POOL_TILE), cols]
            rs = part if rs is None else rs + part
        segs = []
        for s in range(POOL_TILE // blk):
            hi, lo = _split_bf16(rs[s * blk:(s + 1) * blk])
            segs.append(_dot(cm_ref[g], hi) + _dot(cm_ref[g], lo))
        box = jnp.concatenate(segs, axis=0)
        cnt = ((jnp.minimum(r + half, n_rows) - jnp.maximum(r - half, 0))
               * (jnp.minimum(c + half, GRID_W) - jnp.maximum(c - half, 0))).astype(F32)
        dd = box / cnt - h_ref[POOL_HALO:POOL_HALO + POOL_TILE, cols]
        y = _pool_group_out(dd, whi_ref, wlo_ref, g)
        o_ref[:, cols] = x_ref[:, cols] + gt[:, cols] * ((y + bp_ref[:, cols]) * ps_ref[:, cols])


def _pool_ctx_kernel(x_ref, mod_ref, nw_ref, whi_ref, wlo_ref, bp_ref, ps_ref, cm_ref, prev_ref, o_ref, *, d, ctx_row):
    del prev_ref
    gc = d // len(POOL_WINDOWS)
    length = x_ref.shape[0]
    sh = mod_ref[ctx_row:ctx_row + 1, 0:d]
    sc = mod_ref[ctx_row:ctx_row + 1, d:2 * d]
    gt = mod_ref[ctx_row:ctx_row + 1, 2 * d:3 * d]
    x = x_ref[...]
    h = _norm_mod(x, nw_ref[...], sc, sh)
    t = lax.broadcasted_iota(jnp.int32, (length, 1), 0)
    for g, win in enumerate(POOL_WINDOWS):
        half = win // 2
        cols = slice(g * gc, (g + 1) * gc)
        hg = h[:, cols]
        hi, lo = _split_bf16(hg)
        box = _dot(cm_ref[g], hi) + _dot(cm_ref[g], lo)
        cnt = (jnp.minimum(t + half, length) - jnp.maximum(t - half, 0)).astype(F32)
        dd = box / cnt - hg
        y = _pool_group_out(dd, whi_ref, wlo_ref, g)
        o_ref[:, cols] = x[:, cols] + gt[:, cols] * ((y + bp_ref[:, cols]) * ps_ref[:, cols])


def _band_matrices(n, period):
    t = np.arange(n)
    out = []
    for win in POOL_WINDOWS:
        half = win // 2
        diff = t[None, :] - t[:, None]
        same = (t[None, :] // period) == (t[:, None] // period)
        out.append(((diff >= -half) & (diff < half) & same).astype(np.float32))
    return jnp.asarray(np.stack(out), dtype=BF16)


def _pool_layer(x, ctx, mod, nw, w_pool, b_pool, pool_scale):
    bsz, n, d = x.shape
    c_len = ctx.shape[1]
    n_groups = len(POOL_WINDOWS)
    gc = d // n_groups
    t_lat, t_ctx = bsz * n, bsz * c_len
    whi, wlo = _split_bf16(w_pool)
    nw2, bp2, ps2 = nw.reshape(1, d), b_pool.reshape(1, d), pool_scale.reshape(1, d)
    nt = n // POOL_TILE
    nh = n // POOL_HALO
    ratio = POOL_TILE // POOL_HALO
    const2 = lambda b, i: (0, 0)
    const3 = lambda b, i: (0, 0, 0)
    lat = pl.pallas_call(
        functools.partial(_pool_lat_kernel, d=d, n_rows=n // GRID_W),
        out_shape=jax.ShapeDtypeStruct((t_lat + t_ctx, d), F32),
        grid=(bsz, nt),
        in_specs=[
            pl.BlockSpec((None, POOL_TILE, d), lambda b, i: (b, i, 0)),
            pl.BlockSpec((None, POOL_HALO, d), lambda b, i: (b, jnp.maximum(ratio * i - 1, 0), 0)),
            pl.BlockSpec((None, POOL_HALO, d), lambda b, i: (b, jnp.minimum(ratio * i + ratio, nh - 1), 0)),
            pl.BlockSpec(mod.shape, const2),
            pl.BlockSpec((1, d), const2),
            pl.BlockSpec((n_groups, gc, gc), const3),
            pl.BlockSpec((n_groups, gc, gc), const3),
            pl.BlockSpec((1, d), const2),
            pl.BlockSpec((1, d), const2),
            pl.BlockSpec((n_groups, 256, 256), const3),
            pl.BlockSpec(memory_space=pl.ANY),
        ],
        out_specs=pl.BlockSpec((POOL_TILE, d), lambda b, i: (b * nt + i, 0)),
        scratch_shapes=[pltpu.VMEM((POOL_TILE + 2 * POOL_HALO, d), F32)],
        input_output_aliases={10: 0},
        compiler_params=_params("arbitrary", "arbitrary"),
        name="pool_lat",
    )(x, x, x, mod, nw2, whi, wlo, bp2, ps2, _band_matrices(256, GRID_W), jnp.zeros((t_lat + t_ctx, d), F32))
    const1 = lambda b: (0, 0)
    const13 = lambda b: (0, 0, 0)
    return pl.pallas_call(
        functools.partial(_pool_ctx_kernel, d=d, ctx_row=bsz),
        out_shape=jax.ShapeDtypeStruct((t_lat + t_ctx, d), F32),
        grid=(bsz,),
        in_specs=[
            pl.BlockSpec((None, c_len, d), lambda b: (b, 0, 0)),
            pl.BlockSpec(mod.shape, const1),
            pl.BlockSpec((1, d), const1),
            pl.BlockSpec((n_groups, gc, gc), const13),
            pl.BlockSpec((n_groups, gc, gc), const13),
            pl.BlockSpec((1, d), const1),
            pl.BlockSpec((1, d), const1),
            pl.BlockSpec((n_groups, c_len, c_len), const13),
            pl.BlockSpec(memory_space=pl.ANY),
        ],
        out_specs=pl.BlockSpec((c_len, d), lambda b: (t_lat // c_len + b, 0)),
        input_output_aliases={8: 0},
        compiler_params=_params("arbitrary"),
        name="pool_ctx",
    )(ctx, mod, nw2, whi, wlo, bp2, ps2, _band_matrices(c_len, c_len), lat)


def _for_rows(n, fn):
    def body(step, carry):
        for u in range(DMA_UNROLL):
            fn(step * DMA_UNROLL + u)
        return carry

    lax.fori_loop(0, n // DMA_UNROLL, body, 0)


def _mod_row(tile, lat_tiles, tiles_per_batch, ctx_row):
    return jnp.where(tile < lat_tiles, tile // tiles_per_batch, ctx_row)


def _route_kernel(x_ref, mod_ref, nw_ref, wr_ref, br_ref, ls_ref, rtok_ref, rt_ref, cnt_ref, carry_ref,
                  *, d, lat_tiles, tiles_per_batch, ctx_row):
    i = pl.program_id(0)
    row = _mod_row(i, lat_tiles, tiles_per_batch, ctx_row)
    sh = mod_ref[pl.ds(row, 1), 3 * d:4 * d]
    sc = mod_ref[pl.ds(row, 1), 4 * d:5 * d]
    h = _norm_mod(x_ref[...], nw_ref[...], sc, sh)
    logits = _dot_hi(h, wr_ref[...]) + br_ref[...]
    lane = lax.broadcasted_iota(jnp.int32, logits.shape, 1).astype(F32)
    neg = jnp.float32(-jnp.inf)
    big = jnp.float32(1e9)
    is_g = lane < N_GROUPS
    lg = jnp.where(is_g, logits, neg)
    gmax = jnp.max(lg, axis=1, keepdims=True)
    g_idx = jnp.min(jnp.where(lg == gmax, lane, big), axis=1, keepdims=True)
    pg_top = 1.0 / jnp.sum(jnp.where(is_g, jnp.exp(lg - gmax), 0.0), axis=1, keepdims=True)
    lo = N_GROUPS + EXPERTS_PER_GROUP * g_idx
    le = jnp.where((lane >= lo) & (lane < lo + EXPERTS_PER_GROUP), logits, neg)
    e1 = jnp.max(le, axis=1, keepdims=True)
    i1 = jnp.min(jnp.where(le == e1, lane, big), axis=1, keepdims=True)
    le2 = jnp.where(lane == i1, neg, le)
    e2 = jnp.max(le2, axis=1, keepdims=True)
    i2 = jnp.min(jnp.where(le2 == e2, lane, big), axis=1, keepdims=True)
    r21 = jnp.exp(e2 - e1)
    w1 = pg_top / (1.0 + r21)
    w2 = pg_top * r21 / (1.0 + r21)
    id1 = i1 - N_GROUPS
    id2 = i2 - N_GROUPS

    @pl.when(i == 0)
    def _():
        carry_ref[...] = jnp.zeros_like(carry_ref)

    oh1 = (lane == id1).astype(F32)
    oh2 = (lane == id2).astype(F32)
    cum1 = _dot(ls_ref[...], oh1.astype(BF16))
    cum2 = _dot(ls_ref[...], oh2.astype(BF16))
    tot1 = jnp.sum(oh1, axis=0, keepdims=True)
    tot2 = jnp.sum(oh2, axis=0, keepdims=True)
    carry = carry_ref[...]
    rank1 = jnp.sum(oh1 * (cum1 + carry), axis=1, keepdims=True)
    rank2 = jnp.sum(oh2 * (cum2 + carry + tot1), axis=1, keepdims=True)
    carry = carry + tot1 + tot2
    carry_ref[...] = carry
    cnt_ref[...] = carry

    fields = (id1, id2, rank1, rank2, w1, w2)
    slab = jnp.zeros_like(logits)
    for k, f in enumerate(fields):
        slab = jnp.where(lane == k, f, slab)
    rtok_ref[...] = slab
    rt_ref[...] = slab.T[0:SUBLANES]


def _dispatch_kernel(pos_ref, x_ref, mod_ref, nw_ref, zero_ref, o_ref, h_ref, sem, *, d, lat_tiles, tiles_per_batch, ctx_row):
    del zero_ref
    i = pl.program_id(0)
    row = _mod_row(i, lat_tiles, tiles_per_batch, ctx_row)
    sh = mod_ref[pl.ds(row, 1), 3 * d:4 * d]
    sc = mod_ref[pl.ds(row, 1), 4 * d:5 * d]
    h_ref[...] = _norm_mod(x_ref[...], nw_ref[...], sc, sh)

    def copy(r, k):
        return pltpu.make_async_copy(h_ref.at[pl.ds(r, 1)], o_ref.at[pl.ds(pos_ref[0, k, r], 1)], sem)

    _for_rows(TOK_TILE, lambda r: (copy(r, 0).start(), copy(r, 1).start()))
    _for_rows(TOK_TILE, lambda r: (copy(r, 0).wait(), copy(r, 1).wait()))


def _gmm_kernel(te_ref, nu_ref, x_ref, wg_ref, wu_ref, wd_ref, o_ref):
    i = pl.program_id(0)

    @pl.when(i < nu_ref[0])
    def _():
        x = x_ref[...].astype(BF16)
        hid = _silu(_dot(x, wg_ref[...])) * _dot(x, wu_ref[...])
        o_ref[...] = _dot(hid.astype(BF16), wd_ref[...])

    @pl.when(i >= nu_ref[0])
    def _():
        o_ref[...] = jnp.zeros_like(o_ref)


def _combine_kernel(pos_ref, x_ref, rtok_ref, mod_ref, nf_ref, y_ref, o_ref, g1_ref, g2_ref, sem,
                    *, d, lat_tiles, tiles_per_batch, ctx_row, final_norm):
    i = pl.program_id(0)
    row = _mod_row(i, lat_tiles, tiles_per_batch, ctx_row)
    gt = mod_ref[pl.ds(row, 1), 5 * d:6 * d]
    bufs = (g1_ref, g2_ref)

    def copy(r, k):
        return pltpu.make_async_copy(y_ref.at[pl.ds(pos_ref[0, k, r], 1)], bufs[k].at[pl.ds(r, 1)], sem)

    _for_rows(TOK_TILE, lambda r: (copy(r, 0).start(), copy(r, 1).start()))
    _for_rows(TOK_TILE, lambda r: (copy(r, 0).wait(), copy(r, 1).wait()))
    rt = rtok_ref[...]
    w1 = rt[:, 4:5]
    w2 = rt[:, 5:6]
    out = x_ref[...] + gt * (w1 * g1_ref[...] + w2 * g2_ref[...])
    if final_norm:
        out = out * lax.rsqrt(jnp.mean(out * out, axis=-1, keepdims=True) + EPS) * nf_ref[...]
    o_ref[...] = out


def _moe_layer(xflat, n_tok, mod, nw, w_rg, b_rg, w_re, b_re, wg, wu, wd, *, t_lat, n, ctx_row, norm_final):
    d = xflat.shape[1]
    n_exp = w_re.shape[1]
    tiles = n_tok // TOK_TILE
    lat_tiles = t_lat // TOK_TILE
    tiles_per_batch = n // TOK_TILE
    common = dict(d=d, lat_tiles=lat_tiles, tiles_per_batch=tiles_per_batch, ctx_row=ctx_row)
    nw2 = nw.reshape(1, d)
    wr = jnp.zeros((d, LANES), F32).at[:, :N_GROUPS].set(w_rg).at[:, N_GROUPS:N_GROUPS + n_exp].set(w_re)
    br = jnp.zeros((1, LANES), F32).at[0, :N_GROUPS].set(b_rg).at[0, N_GROUPS:N_GROUPS + n_exp].set(b_re)
    lstrict = jnp.asarray(np.tril(np.ones((TOK_TILE, TOK_TILE), np.float32), -1), dtype=BF16)
    const = lambda i: (0, 0)
    rtok, rt, cnt = pl.pallas_call(
        functools.partial(_route_kernel, **common),
        out_shape=(jax.ShapeDtypeStruct((n_tok, LANES), F32),
                   jax.ShapeDtypeStruct((SUBLANES, n_tok), F32),
                   jax.ShapeDtypeStruct((1, LANES), F32)),
        grid=(tiles,),
        in_specs=[
            pl.BlockSpec((TOK_TILE, d), lambda i: (i, 0)),
            pl.BlockSpec(mod.shape, const),
            pl.BlockSpec((1, d), const),
            pl.BlockSpec((d, LANES), const),
            pl.BlockSpec((1, LANES), const),
            pl.BlockSpec((TOK_TILE, TOK_TILE), const),
        ],
        out_specs=(pl.BlockSpec((TOK_TILE, LANES), lambda i: (i, 0)),
                   pl.BlockSpec((SUBLANES, TOK_TILE), lambda i: (0, i)),
                   pl.BlockSpec((1, LANES), const)),
        scratch_shapes=[pltpu.VMEM((1, LANES), F32)],
        compiler_params=_params("arbitrary"),
        name="moe_route",
    )(xflat, mod, nw2, wr, br, lstrict)

    counts = cnt[0, :n_exp].astype(jnp.int32)
    padded = ((counts + MOE_TILE - 1) // MOE_TILE) * MOE_TILE
    ends = jnp.cumsum(padded)
    offsets = ends - padded
    n_sorted_tiles = (2 * n_tok) // MOE_TILE + n_exp
    n_used = (ends[-1] // MOE_TILE).astype(jnp.int32).reshape(1)
    tile_ids = jnp.arange(n_sorted_tiles, dtype=jnp.int32)
    tile_expert = jnp.sum(tile_ids[:, None] * MOE_TILE >= ends[None, :], axis=1).astype(jnp.int32)
    last_expert = jnp.sum(jnp.maximum(n_used - 1, 0) * MOE_TILE >= ends).astype(jnp.int32)
    tile_expert = jnp.minimum(tile_expert, last_expert)
    ids = rt[0:2].astype(jnp.int32)
    onehot = ids[:, :, None] == jnp.arange(n_exp, dtype=jnp.int32)
    pos = jnp.sum(jnp.where(onehot, offsets, 0), axis=-1) + rt[2:4].astype(jnp.int32)
    pos3 = pos.reshape(2, tiles, TOK_TILE).transpose(1, 0, 2)

    n_sorted = n_sorted_tiles * MOE_TILE
    pos_spec = pl.BlockSpec((1, 2, TOK_TILE), lambda i: (i, 0, 0), memory_space=pltpu.SMEM)
    xs_sorted = pl.pallas_call(
        functools.partial(_dispatch_kernel, **common),
        out_shape=jax.ShapeDtypeStruct((n_sorted, d), F32),
        grid=(tiles,),
        in_specs=[
            pos_spec,
            pl.BlockSpec((TOK_TILE, d), lambda i: (i, 0)),
            pl.BlockSpec(mod.shape, const),
            pl.BlockSpec((1, d), const),
            pl.BlockSpec(memory_space=pl.ANY),
        ],
        out_specs=pl.BlockSpec(memory_space=pl.ANY),
        scratch_shapes=[pltpu.VMEM((TOK_TILE, d), F32), pltpu.SemaphoreType.DMA(())],
        input_output_aliases={4: 0},
        compiler_params=_params("arbitrary"),
        name="moe_dispatch",
    )(pos3, xflat, mod, nw2, jnp.zeros((n_sorted, d), F32))

    de = wg.shape[-1]
    y_sorted = pl.pallas_call(
        _gmm_kernel,
        out_shape=jax.ShapeDtypeStruct((n_sorted, d), F32),
        grid_spec=pltpu.PrefetchScalarGridSpec(
            num_scalar_prefetch=2,
            grid=(n_sorted_tiles,),
            in_specs=[
                pl.BlockSpec((MOE_TILE, d), lambda i, te, nu: (jnp.minimum(i, jnp.maximum(nu[0] - 1, 0)), 0)),
                pl.BlockSpec((None, d, de), lambda i, te, nu: (te[i], 0, 0)),
                pl.BlockSpec((None, d, de), lambda i, te, nu: (te[i], 0, 0)),
                pl.BlockSpec((None, de, d), lambda i, te, nu: (te[i], 0, 0)),
            ],
            out_specs=pl.BlockSpec((MOE_TILE, d), lambda i, te, nu: (i, 0)),
        ),
        compiler_params=_params("arbitrary"),
        name="moe_gmm",
    )(tile_expert, n_used, xs_sorted, wg.astype(BF16), wu.astype(BF16), wd.astype(BF16))

    final = norm_final is not None
    nf = (norm_final if final else jnp.ones((d,), F32)).reshape(1, d)
    return pl.pallas_call(
        functools.partial(_combine_kernel, final_norm=final, **common),
        out_shape=jax.ShapeDtypeStruct((n_tok, d), F32),
        grid=(tiles,),
        in_specs=[
            pos_spec,
            pl.BlockSpec((TOK_TILE, d), lambda i: (i, 0)),
            pl.BlockSpec((TOK_TILE, LANES), lambda i: (i, 0)),
            pl.BlockSpec(mod.shape, const),
            pl.BlockSpec((1, d), const),
            pl.BlockSpec(memory_space=pl.ANY),
        ],
        out_specs=pl.BlockSpec((TOK_TILE, d), lambda i: (i, 0)),
        scratch_shapes=[pltpu.VMEM((TOK_TILE, d), F32), pltpu.VMEM((TOK_TILE, d), F32), pltpu.SemaphoreType.DMA(())],
        compiler_params=_params("arbitrary"),
        name="moe_combine",
    )(pos3, xflat, rtok, mod, nf, y_sorted)


def _dn_proj_kernel(x_ref, xp_ref, xn_ref, mod_ref, nw_ref, wqkv_ref, wz_ref, wab_ref, wconv_ref, gpar_ref,
                    mpre_ref, msuf_ref, q_ref, k_ref, v_ref, z_ref, gct_ref, gcr_ref, hf_ref,
                    *, d, tile, tiles_per_seq, lat, ctx_row):
    i = pl.program_id(0)
    row = (i // tiles_per_seq) if lat else ctx_row
    sh = mod_ref[pl.ds(row, 1), 0:d]
    sc = mod_ref[pl.ds(row, 1), d:2 * d]
    nw = nw_ref[...]
    first = (i % tiles_per_seq) == 0
    last = (i % tiles_per_seq) == tiles_per_seq - 1
    halo = SUBLANES
    hf_ref[0:halo] = jnp.where(first, 0.0, _norm_mod(xp_ref[...], nw, sc, sh))
    hm = _norm_mod(x_ref[...], nw, sc, sh)
    hf_ref[halo:halo + tile] = hm
    hf_ref[halo + tile:] = jnp.where(last, 0.0, _norm_mod(xn_ref[...], nw, sc, sh))
    hb = hf_ref[...].astype(BF16)

    z_ref[...] = _dot(hm.astype(BF16), wz_ref[...])

    ab = _dot_hi(hm, wab_ref[...])
    a_log, dt_bias, is_g, is_f, is_b = (gpar_ref[j:j + 1, :] for j in range(5))
    xg = ab + dt_bias
    softplus = jnp.maximum(xg, 0.0) + jnp.log(1.0 + jnp.exp(-jnp.abs(xg)))
    slab = jnp.where(is_g > 0.5, -jnp.exp(a_log) * softplus, jax.nn.sigmoid(ab))
    gpre = _dot_hi(mpre_ref[...], slab)
    gsuf = _dot_hi(msuf_ref[...], slab)
    gcs = jnp.where(is_f > 0.5, gpre, jnp.where(is_b > 0.5, gsuf, slab))
    gct_ref[...] = gcs
    gcs_t = gcs.T
    n_ab = 4 * DN_HEADS
    for cc in range(tile // DN_CHUNK):
        gcr_ref[cc] = gcs_t[0:n_ab, cc * DN_CHUNK:(cc + 1) * DN_CHUNK]

    dk = d // DN_HEADS
    cw = 4 * dk
    outs = (q_ref, k_ref, v_ref)
    for cc in range(3 * d // cw):
        cols = slice(cc * cw, (cc + 1) * cw)
        pc = _dot(hb, wqkv_ref[:, cols])
        wc = wconv_ref[:, cols]
        n_rows = tile + 2 * halo
        conv = (pltpu.roll(pc, 2, 0)[halo:halo + tile] * wc[0:1]
                + pltpu.roll(pc, 1, 0)[halo:halo + tile] * wc[1:2]
                + pc[halo:halo + tile] * wc[2:3]
                + pltpu.roll(pc, n_rows - 1, 0)[halo:halo + tile] * wc[3:4])
        act = _silu(conv)
        which = (cc * cw) // d
        for hh in range(cw // dk):
            head = ((cc * cw) % d) // dk + hh
            t = act[:, hh * dk:(hh + 1) * dk]
            if which < 2:
                t = t * lax.rsqrt(jnp.sum(t * t, axis=-1, keepdims=True) + EPS)
            if which == 0:
                t = t * (dk ** -0.5)
            outs[which][head] = t


def _dn_proj(xflat, tok_off, n_seq, seq_len, tile, lat, mod, nw, wqkv, wz, wab, wconv, gpar, ctx_row):
    d = xflat.shape[1]
    dk = d // DN_HEADS
    t_n = n_seq * seq_len
    tiles = t_n // tile
    tiles_per_seq = seq_len // tile
    boff = tok_off // tile
    hoff = tok_off // SUBLANES
    hper = tile // SUBLANES
    hmax = xflat.shape[0] // SUBLANES - 1
    idx = np.arange(tile)
    same = (idx[:, None] // DN_CHUNK) == (idx[None, :] // DN_CHUNK)
    mpre = jnp.asarray((same & (idx[None, :] <= idx[:, None])).astype(np.float32))
    msuf = jnp.asarray((same & (idx[None, :] >= idx[:, None])).astype(np.float32))
    const = lambda i: (0, 0)
    hshape = jax.ShapeDtypeStruct((DN_HEADS, t_n, dk), F32)
    hspec = pl.BlockSpec((DN_HEADS, tile, dk), lambda i: (0, i, 0))
    n_ab = 4 * DN_HEADS
    return pl.pallas_call(
        functools.partial(_dn_proj_kernel, d=d, tile=tile, tiles_per_seq=tiles_per_seq, lat=lat, ctx_row=ctx_row),
        out_shape=(hshape, hshape, hshape,
                   jax.ShapeDtypeStruct((t_n, d), F32),
                   jax.ShapeDtypeStruct((t_n, LANES), F32),
                   jax.ShapeDtypeStruct((t_n // DN_CHUNK, n_ab, DN_CHUNK), F32)),
        grid=(tiles,),
        in_specs=[
            pl.BlockSpec((tile, d), lambda i: (boff + i, 0)),
            pl.BlockSpec((SUBLANES, d), lambda i: (jnp.maximum(hoff + i * hper - 1, 0), 0)),
            pl.BlockSpec((SUBLANES, d), lambda i: (jnp.minimum(hoff + (i + 1) * hper, hmax), 0)),
            pl.BlockSpec(mod.shape, const),
            pl.BlockSpec((1, d), const),
            pl.BlockSpec(wqkv.shape, const),
            pl.BlockSpec(wz.shape, const),
            pl.BlockSpec(wab.shape, const),
            pl.BlockSpec(wconv.shape, const),
            pl.BlockSpec(gpar.shape, const),
            pl.BlockSpec((tile, tile), const),
            pl.BlockSpec((tile, tile), const),
        ],
        out_specs=(hspec, hspec, hspec,
                   pl.BlockSpec((tile, d), lambda i: (i, 0)),
                   pl.BlockSpec((tile, LANES), lambda i: (i, 0)),
                   pl.BlockSpec((tile // DN_CHUNK, n_ab, DN_CHUNK), lambda i: (i, 0, 0))),
        scratch_shapes=[pltpu.VMEM((tile + 2 * SUBLANES, d), F32)],
        compiler_params=_params("arbitrary"),
        name="dn_proj_lat" if lat else "dn_proj_ctx",
    )(xflat, xflat, xflat, mod, nw.reshape(1, d), wqkv, wz, wab, wconv, gpar, mpre, msuf)


def _dn_chunk_kernel(q_ref, k_ref, v_ref, gct_ref, gcr_ref, s0_ref, *rest, blk, reverse, need_output):
    if need_output:
        o_ref, sf_ref, s_ref, u_s, wq_s, kdt_s, qk_s = rest
    else:
        sf_ref, s_ref, u_s, wq_s, kdt_s = rest
        o_ref = qk_s = None
    j = pl.program_id(1)
    nchunk = blk // DN_CHUNK
    c = DN_CHUNK
    dv = v_ref.shape[-1]
    heads = range(DN_HEADS)

    @pl.when(j == 0)
    def _():
        s_ref[...] = s0_ref[...]

    ii = lax.broadcasted_iota(jnp.int32, (c, c), 0)
    jj = lax.broadcasted_iota(jnp.int32, (c, c), 1)
    strict = (ii < jj) if reverse else (ii > jj)
    incl = (ii <= jj) if reverse else (ii >= jj)
    field = 2 if reverse else 0
    nt_dims = (((1,), (1,)), ((), ()))

    def g_last(gcc):
        return gcc[0:1, :] if reverse else gcc[c - 1:c, :]

    def prep(step, carry):
        inst = []
        for cc in range(DN_PREP_CHUNKS):
            ci = step * DN_PREP_CHUNKS + cc
            rows = pl.ds(pl.multiple_of(ci * c, c), c)
            gcols = gct_ref[rows, :]
            grows = gcr_ref[ci]
            for h in heads:
                ln = 4 * h + field
                inst.append((ci, rows, h, gcols[:, ln:ln + 1], gcols[:, ln + 1:ln + 2], grows[ln:ln + 1, :]))
        kc = [k_ref[h, rows, :] for _, rows, h, _, _, _ in inst]
        vc = [v_ref[h, rows, :] for _, rows, h, _, _, _ in inst]
        decay = [jnp.exp(jnp.where(incl, gcc - gcr, 0.0)) for _, _, _, gcc, _, gcr in inst]
        egc = [jnp.exp(t[3]) for t in inst]
        kb = [kc_i * t[4] for kc_i, t in zip(kc, inst)]
        if need_output:
            qc = [q_ref[h, rows, :] for _, rows, h, _, _, _ in inst]
            lhs = [jnp.concatenate([kb_i, qc_i], axis=0).astype(BF16) for kb_i, qc_i in zip(kb, qc)]
        else:
            lhs = [kb_i.astype(BF16) for kb_i in kb]
        kq = [lax.dot_general(l_i, kc_i.astype(BF16), nt_dims, preferred_element_type=F32)
              for l_i, kc_i in zip(lhs, kc)]
        pb = [jnp.where(strict, -(kq_i[0:c] * d_i), 0.0).astype(BF16) for kq_i, d_i in zip(kq, decay)]
        rhs = [jnp.concatenate([vc_i * t[4], kb_i * e_i], axis=1) for vc_i, kb_i, e_i, t in zip(vc, kb, egc, inst)]
        x = [r_i + _dot(p_i, r_i.astype(BF16)) for p_i, r_i in zip(pb, rhs)]
        for _ in range(5):
            pb = [_dot(p_i, p_i).astype(BF16) for p_i in pb]
            x = [x_i + _dot(p_i, x_i.astype(BF16)) for p_i, x_i in zip(pb, x)]
        for n_i, (ci, rows, h, gcc, _, _) in enumerate(inst):
            u_s[h, rows, :] = x[n_i][:, 0:dv]
            kdt_s[h, ci] = (kc[n_i] * jnp.exp(g_last(gcc) - gcc)).T.astype(BF16)
            if need_output:
                wq_s[h, ci] = jnp.concatenate([x[n_i][:, dv:], qc[n_i] * egc[n_i]], axis=0).astype(BF16)
                qk_s[h, rows, :] = jnp.where(incl, kq[n_i][c:2 * c] * decay[n_i], 0.0).astype(BF16)
            else:
                wq_s[h, ci] = x[n_i][:, dv:].astype(BF16)
        return carry

    lax.fori_loop(0, nchunk // DN_PREP_CHUNKS, prep, 0)

    def recur(step, carry):
        ci = (nchunk - 1 - step) if reverse else step
        rows = pl.ds(pl.multiple_of(ci * c, c), c)
        gcols = gct_ref[rows, :]
        s = [s_ref[h] for h in heads]
        sb = [s_h.astype(BF16) for s_h in s]
        ws = [_dot(wq_s[h, ci], sb[h]) for h in heads]
        vb = [(u_s[h, rows, :] - ws[h][0:c]).astype(BF16) for h in heads]
        if need_output:
            for h in heads:
                o_ref[h, rows, :] = ws[h][c:2 * c] + _dot(qk_s[h, rows, :], vb[h])
        for h in heads:
            eg = jnp.exp(g_last(gcols[:, 4 * h + field:4 * h + field + 1]))
            s_ref[h] = s[h] * eg + _dot(kdt_s[h, ci], vb[h])
        return carry

    lax.fori_loop(0, nchunk, recur, 0)

    @pl.when(j == pl.num_programs(1) - 1)
    def _():
        sf_ref[...] = s_ref[...]


def _dn_scan(q, k, v, gct, gcr, s0, n_seq, seq_len, blk, reverse, need_output):
    dk = q.shape[-1]
    nblk = seq_len // blk
    order = (lambda j: nblk - 1 - j) if reverse else (lambda j: j)
    hspec = pl.BlockSpec((DN_HEADS, blk, dk), lambda b, j: (0, b * nblk + order(j), 0))
    sspec = pl.BlockSpec((None, DN_HEADS, dk, dk), lambda b, j: (b, 0, 0, 0))
    n_ab = gcr.shape[1]
    s_shape = jax.ShapeDtypeStruct((n_seq, DN_HEADS, dk, dk), F32)
    out_shape = [s_shape]
    out_specs = [sspec]
    nchunk = blk // DN_CHUNK
    wq_rows = 2 * DN_CHUNK if need_output else DN_CHUNK
    scratch = [pltpu.VMEM((DN_HEADS, dk, dk), F32),
               pltpu.VMEM((DN_HEADS, blk, dk), F32),
               pltpu.VMEM((DN_HEADS, nchunk, wq_rows, dk), BF16),
               pltpu.VMEM((DN_HEADS, nchunk, dk, DN_CHUNK), BF16)]
    if need_output:
        out_shape.insert(0, jax.ShapeDtypeStruct(q.shape, F32))
        out_specs.insert(0, hspec)
        scratch.append(pltpu.VMEM((DN_HEADS, blk, DN_CHUNK), BF16))
    res = pl.pallas_call(
        functools.partial(_dn_chunk_kernel, blk=blk, reverse=reverse, need_output=need_output),
        out_shape=tuple(out_shape),
        grid=(n_seq, nblk),
        in_specs=[
            hspec, hspec, hspec,
            pl.BlockSpec((blk, LANES), lambda b, j: (b * nblk + order(j), 0)),
            pl.BlockSpec((blk // DN_CHUNK, n_ab, DN_CHUNK), lambda b, j: (b * nblk + order(j), 0, 0)),
            sspec,
        ],
        out_specs=tuple(out_specs),
        scratch_shapes=scratch,
        compiler_params=_params("arbitrary", "arbitrary"),
        name="dn_scan_" + ("bwd" if reverse else "fwd") + ("_out" if need_output else "_state"),
    )(q, k, v, gct, gcr, s0)
    return (res[0], res[1]) if need_output else (None, res[0])


def _dn_out_kernel(of_ref, ob_ref, z_ref, x_ref, mod_ref, nrm_ref, wout_ref, o_ref, *, d, tiles_per_batch):
    i = pl.program_id(0)
    row = i // tiles_per_batch
    gt = mod_ref[pl.ds(row, 1), 2 * d:3 * d]
    dk = d // DN_HEADS
    parts = []
    for h in range(DN_HEADS):
        o = of_ref[h] + ob_ref[h]
        o = o * lax.rsqrt(jnp.mean(o * o, axis=-1, keepdims=True) + EPS) * nrm_ref[...]
        parts.append((o * _silu(z_ref[:, h * dk:(h + 1) * dk])).astype(BF16))
    y = _dot(jnp.concatenate(parts, axis=1), wout_ref[...])
    o_ref[...] = x_ref[...] + gt * y


def _deltanet_layer(xflat, bsz, n, c_len, mod, nw, w_in, w_conv, a_log, dt_bias, dn_norm, w_out):
    d = xflat.shape[1]
    dk = d // DN_HEADS
    t_lat = bsz * n
    ctx_row = bsz
    wqkv = w_in[:, :3 * d].astype(BF16)
    wz = w_in[:, 3 * d:4 * d].astype(BF16)
    n_ab = 4 * DN_HEADS
    wab = w_in[:, 4 * d:].reshape(d, 4, DN_HEADS).transpose(0, 2, 1).reshape(d, n_ab)
    wab = jnp.zeros((d, LANES), F32).at[:, :n_ab].set(wab)
    lane = np.arange(LANES)
    fld, head, used = lane % 4, np.minimum(lane // 4, DN_HEADS - 1), lane < n_ab
    is_g = used & (fld % 2 == 0)
    dirn = fld // 2
    gpar = jnp.zeros((SUBLANES, LANES), F32)
    gpar = gpar.at[0].set(jnp.where(is_g, a_log[dirn, head], 0.0))
    gpar = gpar.at[1].set(jnp.where(is_g, dt_bias[dirn, head], 0.0))
    gpar = gpar.at[2].set(jnp.asarray(is_g, F32))
    gpar = gpar.at[3].set(jnp.asarray(used & (fld == 0), F32))
    gpar = gpar.at[4].set(jnp.asarray(used & (fld == 2), F32))

    proj = functools.partial(_dn_proj, xflat, mod=mod, nw=nw, wqkv=wqkv, wz=wz, wab=wab, wconv=w_conv, gpar=gpar,
                             ctx_row=ctx_row)
    q, k, v, z, gct, gcr = proj(tok_off=0, n_seq=bsz, seq_len=n, tile=TOK_TILE, lat=True)
    qc, kc, vc, _, gctc, gcrc = proj(tok_off=t_lat, n_seq=bsz, seq_len=c_len, tile=c_len, lat=False)

    s0 = jnp.zeros((bsz, DN_HEADS, dk, dk), F32)
    o_dir = []
    for reverse in (False, True):
        _, s_ctx = _dn_scan(qc, kc, vc, gctc, gcrc, s0, bsz, c_len, c_len, reverse, False)
        o, _ = _dn_scan(q, k, v, gct, gcr, s_ctx, bsz, n, TOK_TILE, reverse, True)
        o_dir.append(o)

    tiles = t_lat // TOK_TILE
    const = lambda i: (0, 0)
    hspec = pl.BlockSpec((DN_HEADS, TOK_TILE, dk), lambda i: (0, i, 0))
    return pl.pallas_call(
        functools.partial(_dn_out_kernel, d=d, tiles_per_batch=n // TOK_TILE),
        out_shape=jax.ShapeDtypeStruct((t_lat, d), F32),
        grid=(tiles,),
        in_specs=[
            hspec, hspec,
            pl.BlockSpec((TOK_TILE, d), lambda i: (i, 0)),
            pl.BlockSpec((TOK_TILE, d), lambda i: (i, 0)),
            pl.BlockSpec(mod.shape, const),
            pl.BlockSpec((1, dk), const),
            pl.BlockSpec((d, d), const),
        ],
        out_specs=pl.BlockSpec((TOK_TILE, d), lambda i: (i, 0)),
        compiler_params=_params("arbitrary"),
        name="dn_out",
    )(o_dir[0], o_dir[1], z, xflat, mod, dn_norm.reshape(1, dk), w_out.astype(BF16))


def kernel(x, c, ctx, c_ctx, w_ada, b_ada, norm_mix, norm_ffn, w_pool, b_pool, pool_scale, w_dn_in, w_dn_conv, dn_a_log, dn_dt_bias, dn_norm, w_dn_out, w_rg, b_rg, w_re, b_re, w_e_gate, w_e_up, w_e_down, norm_final):
    bsz, n, d = x.shape
    c_len = ctx.shape[1]
    assert w_ada.shape[0] == 2 and bsz < COND_ROWS and n % POOL_TILE == 0 and c_len == 256
    assert (bsz * c_len) % TOK_TILE == 0 and d % (4 * LANES) == 0
    t_lat = bsz * n
    t_all = t_lat + bsz * c_len
    ctx_row = bsz
    cond = jnp.zeros((COND_ROWS, d), F32).at[:bsz].set(c.astype(F32)).at[ctx_row].set(c_ctx.astype(F32))
    mod = _ada(cond, w_ada, b_ada)

    xflat = _pool_layer(x.astype(F32), ctx.astype(F32), mod[0], norm_mix[0], w_pool[0], b_pool[0], pool_scale[0])
    xflat = _moe_layer(xflat, t_all, mod[0], norm_ffn[0], w_rg[0], b_rg[0], w_re[0], b_re[0],
                       w_e_gate[0], w_e_up[0], w_e_down[0], t_lat=t_lat, n=n, ctx_row=ctx_row, norm_final=None)

    xlat = _deltanet_layer(xflat, bsz, n, c_len, mod[1], norm_mix[1], w_dn_in[0], w_dn_conv[0], dn_a_log[0],
                           dn_dt_bias[0], dn_norm[0], w_dn_out[0])
    out = _moe_layer(xlat, t_lat, mod[1], norm_ffn[1], w_rg[1], b_rg[1], w_re[1], b_re[1],
                     w_e_gate[1], w_e_up[1], w_e_down[1], t_lat=t_lat, n=n, ctx_row=ctx_row, norm_final=norm_final)
    return out.reshape(bsz, n, d).astype(x.dtype)
```

```python
import functools

import numpy as np
import jax
import jax.numpy as jnp
from jax import lax
from jax.experimental import pallas as pl
from jax.experimental.pallas import tpu as pltpu

GRID_W = 64
POOL_WINDOWS = (2, 4, 8, 16)
DN_HEADS = 8
DN_CONV_W = 4
DN_CHUNK = 64
DN_PREP_CHUNKS = 2
N_GROUPS = 4
EXPERTS_PER_GROUP = 8
EPS = 1e-6

LANES = 128
SUBLANES = 8
VMEM_LIMIT_BYTES = 56 * 1024 * 1024

COND_ROWS = 8
TOK_TILE = 512
POOL_TILE = 1024
POOL_HALO = 512
MOE_TILE = 512
DMA_UNROLL = 8
BF16 = jnp.bfloat16
F32 = jnp.float32
HIGHEST = lax.Precision.HIGHEST


def _dot(a, b):
    return jnp.dot(a, b, preferred_element_type=F32)


def _dot_hi(a, b):
    return jnp.dot(a, b, precision=HIGHEST, preferred_element_type=F32)


def _split_bf16(a):
    hi = a.astype(BF16)
    lo = (a - hi.astype(F32)).astype(BF16)
    return hi, lo


def _dot3(a, b_hi, b_lo):
    a_hi, a_lo = _split_bf16(a)
    return _dot(a_hi, b_hi) + _dot(a_lo, b_hi) + _dot(a_hi, b_lo)


def _params(*sem):
    return pltpu.CompilerParams(dimension_semantics=sem, vmem_limit_bytes=VMEM_LIMIT_BYTES)


def _norm_mod(x, nw, sc, sh):
    return x * lax.rsqrt(jnp.mean(x * x, axis=-1, keepdims=True) + EPS) * nw * (1.0 + sc) + sh


def _silu(x):
    return x * jax.nn.sigmoid(x)


def _ada_kernel(cond_ref, w_ref, b_ref, o_ref):
    o_ref[...] = _dot_hi(_silu(cond_ref[...]), w_ref[...]) + b_ref[...]


def _ada(cond, w_ada, b_ada):
    depth, d, d6 = w_ada.shape
    tn = d6 // 4
    return pl.pallas_call(
        _ada_kernel,
        out_shape=jax.ShapeDtypeStruct((depth, COND_ROWS, d6), F32),
        grid=(depth, d6 // tn),
        in_specs=[
            pl.BlockSpec((COND_ROWS, d), lambda i, j: (0, 0)),
            pl.BlockSpec((None, d, tn), lambda i, j: (i, 0, j)),
            pl.BlockSpec((None, 1, tn), lambda i, j: (i, 0, j)),
        ],
        out_specs=pl.BlockSpec((None, COND_ROWS, tn), lambda i, j: (i, 0, j)),
        compiler_params=_params("arbitrary", "arbitrary"),
        name="ada",
    )(cond, w_ada, b_ada.reshape(depth, 1, d6))


def _pool_group_out(d, whi_ref, wlo_ref, g):
    dhi, dlo = _split_bf16(d)
    return _dot(dhi, whi_ref[g]) + _dot(dlo, whi_ref[g]) + _dot(dhi, wlo_ref[g])


def _pool_lat_kernel(x_ref, xp_ref, xn_ref, mod_ref, nw_ref, whi_ref, wlo_ref, bp_ref, ps_ref, cm_ref, zero_ref,
                     o_ref, h_ref, *, d, n_rows):
    del zero_ref
    b = pl.program_id(0)
    i = pl.program_id(1)
    nt = pl.num_programs(1)
    gc = d // len(POOL_WINDOWS)
    sh = mod_ref[pl.ds(b, 1), 0:d]
    sc = mod_ref[pl.ds(b, 1), d:2 * d]
    gt = mod_ref[pl.ds(b, 1), 2 * d:3 * d]
    nw = nw_ref[...]
    h_ref[0:POOL_HALO] = jnp.where(i > 0, _norm_mod(xp_ref[...], nw, sc, sh), 0.0)
    h_ref[POOL_HALO:POOL_HALO + POOL_TILE] = _norm_mod(x_ref[...], nw, sc, sh)
    h_ref[POOL_HALO + POOL_TILE:] = jnp.where(i < nt - 1, _norm_mod(xn_ref[...], nw, sc, sh), 0.0)

    tok = lax.broadcasted_iota(jnp.int32, (POOL_TILE, 1), 0)
    r = i * (POOL_TILE // GRID_W) + (tok >> 6)
    c = tok & (GRID_W - 1)
    blk = cm_ref.shape[-1]
    for g, win in enumerate(POOL_WINDOWS):
        half = win // 2
        cols = slice(g * gc, (g + 1) * gc)
        rs = None
        for o in range(-half, half):
            part = h_ref[pl.ds(POOL_HALO + o * GRID_W, POOL_TILE), cols]
            rs = part if rs is None else rs + part
        segs = []
        for s in range(POOL_TILE // blk):
            hi, lo = _split_bf16(rs[s * blk:(s + 1) * blk])
            segs.append(_dot(cm_ref[g], hi) + _dot(cm_ref[g], lo))
        box = jnp.concatenate(segs, axis=0)
        cnt = ((jnp.minimum(r + half, n_rows) - jnp.maximum(r - half, 0))
               * (jnp.minimum(c + half, GRID_W) - jnp.maximum(c - half, 0))).astype(F32)
        dd = box / cnt - h_ref[POOL_HALO:POOL_HALO + POOL_TILE, cols]
        y = _pool_group_out(dd, whi_ref, wlo_ref, g)
        o_ref[:, cols] = x_ref[:, cols] + gt[:, cols] * ((y + bp_ref[:, cols]) * ps_ref[:, cols])


def _pool_ctx_kernel(x_ref, mod_ref, nw_ref, whi_ref, wlo_ref, bp_ref, ps_ref, cm_ref, prev_ref, o_ref, *, d, ctx_row):
    del prev_ref
    gc = d // len(POOL_WINDOWS)
    length = x_ref.shape[0]
    sh = mod_ref[ctx_row:ctx_row + 1, 0:d]
    sc = mod_ref[ctx_row:ctx_row + 1, d:2 * d]
    gt = mod_ref[ctx_row:ctx_row + 1, 2 * d:3 * d]
    x = x_ref[...]
    h = _norm_mod(x, nw_ref[...], sc, sh)
    t = lax.broadcasted_iota(jnp.int32, (length, 1), 0)
    for g, win in enumerate(POOL_WINDOWS):
        half = win // 2
        cols = slice(g * gc, (g + 1) * gc)
        hg = h[:, cols]
        hi, lo = _split_bf16(hg)
        box = _dot(cm_ref[g], hi) + _dot(cm_ref[g], lo)
        cnt = (jnp.minimum(t + half, length) - jnp.maximum(t - half, 0)).astype(F32)
        dd = box / cnt - hg
        y = _pool_group_out(dd, whi_ref, wlo_ref, g)
        o_ref[:, cols] = x[:, cols] + gt[:, cols] * ((y + bp_ref[:, cols]) * ps_ref[:, cols])


def _band_matrices(n, period):
    t = np.arange(n)
    out = []
    for win in POOL_WINDOWS:
        half = win // 2
        diff = t[None, :] - t[:, None]
        same = (t[None, :] // period) == (t[:, None] // period)
        out.append(((diff >= -half) & (diff < half) & same).astype(np.float32))
    return jnp.asarray(np.stack(out), dtype=BF16)


def _pool_layer(x, ctx, mod, nw, w_pool, b_pool, pool_scale):
    bsz, n, d = x.shape
    c_len = ctx.shape[1]
    n_groups = len(POOL_WINDOWS)
    gc = d // n_groups
    t_lat, t_ctx = bsz * n, bsz * c_len
    whi, wlo = _split_bf16(w_pool)
    nw2, bp2, ps2 = nw.reshape(1, d), b_pool.reshape(1, d), pool_scale.reshape(1, d)
    nt = n // POOL_TILE
    nh = n // POOL_HALO
    ratio = POOL_TILE // POOL_HALO
    const2 = lambda b, i: (0, 0)
    const3 = lambda b, i: (0, 0, 0)
    lat = pl.pallas_call(
        functools.partial(_pool_lat_kernel, d=d, n_rows=n // GRID_W),
        out_shape=jax.ShapeDtypeStruct((t_lat + t_ctx, d), F32),
        grid=(bsz, nt),
        in_specs=[
            pl.BlockSpec((None, POOL_TILE, d), lambda b, i: (b, i, 0)),
            pl.BlockSpec((None, POOL_HALO, d), lambda b, i: (b, jnp.maximum(ratio * i - 1, 0), 0)),
            pl.BlockSpec((None, POOL_HALO, d), lambda b, i: (b, jnp.minimum(ratio * i + ratio, nh - 1), 0)),
            pl.BlockSpec(mod.shape, const2),
            pl.BlockSpec((1, d), const2),
            pl.BlockSpec((n_groups, gc, gc), const3),
            pl.BlockSpec((n_groups, gc, gc), const3),
            pl.BlockSpec((1, d), const2),
            pl.BlockSpec((1, d), const2),
            pl.BlockSpec((n_groups, 256, 256), const3),
            pl.BlockSpec(memory_space=pl.ANY),
        ],
        out_specs=pl.BlockSpec((POOL_TILE, d), lambda b, i: (b * nt + i, 0)),
        scratch_shapes=[pltpu.VMEM((POOL_TILE + 2 * POOL_HALO, d), F32)],
        input_output_aliases={10: 0},
        compiler_params=_params("arbitrary", "arbitrary"),
        name="pool_lat",
    )(x, x, x, mod, nw2, whi, wlo, bp2, ps2, _band_matrices(256, GRID_W), jnp.zeros((t_lat + t_ctx, d), F32))
    const1 = lambda b: (0, 0)
    const13 = lambda b: (0, 0, 0)
    return pl.pallas_call(
        functools.partial(_pool_ctx_kernel, d=d, ctx_row=bsz),
        out_shape=jax.ShapeDtypeStruct((t_lat + t_ctx, d), F32),
        grid=(bsz,),
        in_specs=[
            pl.BlockSpec((None, c_len, d), lambda b: (b, 0, 0)),
            pl.BlockSpec(mod.shape, const1),
            pl.BlockSpec((1, d), const1),
            pl.BlockSpec((n_groups, gc, gc), const13),
            pl.BlockSpec((n_groups, gc, gc), const13),
            pl.BlockSpec((1, d), const1),
            pl.BlockSpec((1, d), const1),
            pl.BlockSpec((n_groups, c_len, c_len), const13),
            pl.BlockSpec(memory_space=pl.ANY),
        ],
        out_specs=pl.BlockSpec((c_len, d), lambda b: (t_lat // c_len + b, 0)),
        input_output_aliases={8: 0},
        compiler_params=_params("arbitrary"),
        name="pool_ctx",
    )(ctx, mod, nw2, whi, wlo, bp2, ps2, _band_matrices(c_len, c_len), lat)


def _for_rows(n, fn):
    def body(step, carry):
        base = pl.multiple_of(step * DMA_UNROLL, DMA_UNROLL)
        for u in range(DMA_UNROLL):
            fn(base + u)
        return carry

    lax.fori_loop(0, n // DMA_UNROLL, body, 0)


def _mod_row(tile, lat_tiles, tiles_per_batch, ctx_row):
    return jnp.where(tile < lat_tiles, tile // tiles_per_batch, ctx_row)


def _route_kernel(x_ref, mod_ref, nw_ref, wrh_ref, wrl_ref, br_ref, ls_ref, rtok_ref, rt_ref, cnt_ref, carry_ref,
                  *, d, lat_tiles, tiles_per_batch, ctx_row):
    i = pl.program_id(0)
    row = _mod_row(i, lat_tiles, tiles_per_batch, ctx_row)
    sh = mod_ref[pl.ds(row, 1), 3 * d:4 * d]
    sc = mod_ref[pl.ds(row, 1), 4 * d:5 * d]
    h = _norm_mod(x_ref[...], nw_ref[...], sc, sh)
    logits = _dot3(h, wrh_ref[...], wrl_ref[...]) + br_ref[...]
    lane = lax.broadcasted_iota(jnp.int32, logits.shape, 1).astype(F32)
    neg = jnp.float32(-jnp.inf)
    big = jnp.float32(1e9)
    is_g = lane < N_GROUPS
    lg = jnp.where(is_g, logits, neg)
    gmax = jnp.max(lg, axis=1, keepdims=True)
    g_idx = jnp.min(jnp.where(lg == gmax, lane, big), axis=1, keepdims=True)
    pg_top = 1.0 / jnp.sum(jnp.where(is_g, jnp.exp(lg - gmax), 0.0), axis=1, keepdims=True)
    lo = N_GROUPS + EXPERTS_PER_GROUP * g_idx
    le = jnp.where((lane >= lo) & (lane < lo + EXPERTS_PER_GROUP), logits, neg)
    e1 = jnp.max(le, axis=1, keepdims=True)
    i1 = jnp.min(jnp.where(le == e1, lane, big), axis=1, keepdims=True)
    le2 = jnp.where(lane == i1, neg, le)
    e2 = jnp.max(le2, axis=1, keepdims=True)
    i2 = jnp.min(jnp.where(le2 == e2, lane, big), axis=1, keepdims=True)
    r21 = jnp.exp(e2 - e1)
    w1 = pg_top / (1.0 + r21)
    w2 = pg_top * r21 / (1.0 + r21)
    id1 = i1 - N_GROUPS
    id2 = i2 - N_GROUPS

    @pl.when(i == 0)
    def _():
        carry_ref[...] = jnp.zeros_like(carry_ref)

    oh1 = (lane == id1).astype(F32)
    oh2 = (lane == id2).astype(F32)
    cum1 = _dot(ls_ref[...], oh1.astype(BF16))
    cum2 = _dot(ls_ref[...], oh2.astype(BF16))
    tot1 = jnp.sum(oh1, axis=0, keepdims=True)
    tot2 = jnp.sum(oh2, axis=0, keepdims=True)
    carry = carry_ref[...]
    rank1 = jnp.sum(oh1 * (cum1 + carry), axis=1, keepdims=True)
    rank2 = jnp.sum(oh2 * (cum2 + carry + tot1), axis=1, keepdims=True)
    carry = carry + tot1 + tot2
    carry_ref[...] = carry
    cnt_ref[...] = carry

    fields = (id1, id2, rank1, rank2, w1, w2)
    slab = jnp.zeros_like(logits)
    for k, f in enumerate(fields):
        slab = jnp.where(lane == k, f, slab)
    rtok_ref[...] = slab
    rt_ref[...] = slab.T[0:SUBLANES]


def _dispatch_kernel(pos_ref, x_ref, mod_ref, nw_ref, zero_ref, o_ref, h_ref, sem, *, d, lat_tiles, tiles_per_batch, ctx_row):
    del zero_ref
    i = pl.program_id(0)
    row = _mod_row(i, lat_tiles, tiles_per_batch, ctx_row)
    sh = mod_ref[pl.ds(row, 1), 3 * d:4 * d]
    sc = mod_ref[pl.ds(row, 1), 4 * d:5 * d]
    h_ref[...] = _norm_mod(x_ref[...], nw_ref[...], sc, sh).reshape(h_ref.shape)

    def copy(r, k):
        return pltpu.make_async_copy(h_ref.at[r], o_ref.at[pos_ref[0, 0, k * TOK_TILE + r]], sem)

    _for_rows(TOK_TILE, lambda r: (copy(r, 0).start(), copy(r, 1).start()))
    _for_rows(TOK_TILE, lambda r: (copy(r, 0).wait(), copy(r, 1).wait()))


def _gmm_kernel(te_ref, nu_ref, x_ref, wg_ref, wu_ref, wd_ref, o_ref, wgb_ref, wub_ref, wdb_ref):
    i = pl.program_id(0)

    @pl.when((i == 0) | (te_ref[i] != te_ref[jnp.maximum(i - 1, 0)]))
    def _():
        wgb_ref[...] = wg_ref[...].astype(BF16)
        wub_ref[...] = wu_ref[...].astype(BF16)
        wdb_ref[...] = wd_ref[...].astype(BF16)

    @pl.when(i < nu_ref[0])
    def _():
        x = x_ref[...].reshape(x_ref.shape[0], wgb_ref.shape[0]).astype(BF16)
        hid = _silu(_dot(x, wgb_ref[...])) * _dot(x, wub_ref[...])
        o_ref[...] = _dot(hid.astype(BF16), wdb_ref[...]).reshape(o_ref.shape)

    @pl.when(i >= nu_ref[0])
    def _():
        o_ref[...] = jnp.zeros_like(o_ref)


def _combine_kernel(pos_ref, x_ref, rtok_ref, mod_ref, nf_ref, y_ref, o_ref, g1_ref, g2_ref, sem,
                    *, d, lat_tiles, tiles_per_batch, ctx_row, final_norm):
    i = pl.program_id(0)
    row = _mod_row(i, lat_tiles, tiles_per_batch, ctx_row)
    gt = mod_ref[pl.ds(row, 1), 5 * d:6 * d]
    bufs = (g1_ref, g2_ref)

    def copy(r, k):
        return pltpu.make_async_copy(y_ref.at[pos_ref[0, 0, k * TOK_TILE + r]], bufs[k].at[r], sem)

    _for_rows(TOK_TILE, lambda r: (copy(r, 0).start(), copy(r, 1).start()))
    _for_rows(TOK_TILE, lambda r: (copy(r, 0).wait(), copy(r, 1).wait()))
    rt = rtok_ref[...]
    w1 = rt[:, 4:5]
    w2 = rt[:, 5:6]
    out = x_ref[...] + gt * (w1 * g1_ref[...].reshape(x_ref.shape) + w2 * g2_ref[...].reshape(x_ref.shape))
    if final_norm:
        out = out * lax.rsqrt(jnp.mean(out * out, axis=-1, keepdims=True) + EPS) * nf_ref[...]
    o_ref[...] = out


def _moe_layer(xflat, n_tok, mod, nw, w_rg, b_rg, w_re, b_re, wg, wu, wd, *, t_lat, n, ctx_row, norm_final):
    d = xflat.shape[1]
    n_exp = w_re.shape[1]
    tiles = n_tok // TOK_TILE
    lat_tiles = t_lat // TOK_TILE
    tiles_per_batch = n // TOK_TILE
    common = dict(d=d, lat_tiles=lat_tiles, tiles_per_batch=tiles_per_batch, ctx_row=ctx_row)
    nw2 = nw.reshape(1, d)
    wr = jnp.zeros((d, LANES), F32).at[:, :N_GROUPS].set(w_rg).at[:, N_GROUPS:N_GROUPS + n_exp].set(w_re)
    br = jnp.zeros((1, LANES), F32).at[0, :N_GROUPS].set(b_rg).at[0, N_GROUPS:N_GROUPS + n_exp].set(b_re)
    wrh, wrl = _split_bf16(wr)
    lstrict = jnp.asarray(np.tril(np.ones((TOK_TILE, TOK_TILE), np.float32), -1), dtype=BF16)
    const = lambda i: (0, 0)
    rtok, rt, cnt = pl.pallas_call(
        functools.partial(_route_kernel, **common),
        out_shape=(jax.ShapeDtypeStruct((n_tok, LANES), F32),
                   jax.ShapeDtypeStruct((SUBLANES, n_tok), F32),
                   jax.ShapeDtypeStruct((1, LANES), F32)),
        grid=(tiles,),
        in_specs=[
            pl.BlockSpec((TOK_TILE, d), lambda i: (i, 0)),
            pl.BlockSpec(mod.shape, const),
            pl.BlockSpec((1, d), const),
            pl.BlockSpec((d, LANES), const),
            pl.BlockSpec((d, LANES), const),
            pl.BlockSpec((1, LANES), const),
            pl.BlockSpec((TOK_TILE, TOK_TILE), const),
        ],
        out_specs=(pl.BlockSpec((TOK_TILE, LANES), lambda i: (i, 0)),
                   pl.BlockSpec((SUBLANES, TOK_TILE), lambda i: (0, i)),
                   pl.BlockSpec((1, LANES), const)),
        scratch_shapes=[pltpu.VMEM((1, LANES), F32)],
        compiler_params=_params("arbitrary"),
        name="moe_route",
    )(xflat, mod, nw2, wrh, wrl, br, lstrict)

    counts = cnt[0, :n_exp].astype(jnp.int32)
    padded = ((counts + MOE_TILE - 1) // MOE_TILE) * MOE_TILE
    ends = jnp.cumsum(padded)
    offsets = ends - padded
    n_sorted_tiles = (2 * n_tok) // MOE_TILE + n_exp
    n_used = (ends[-1] // MOE_TILE).astype(jnp.int32).reshape(1)
    tile_ids = jnp.arange(n_sorted_tiles, dtype=jnp.int32)
    tile_expert = jnp.sum(tile_ids[:, None] * MOE_TILE >= ends[None, :], axis=1).astype(jnp.int32)
    last_expert = jnp.sum(jnp.maximum(n_used - 1, 0) * MOE_TILE >= ends).astype(jnp.int32)
    tile_expert = jnp.minimum(tile_expert, last_expert)
    ids = rt[0:2].astype(jnp.int32)
    onehot = ids[:, :, None] == jnp.arange(n_exp, dtype=jnp.int32)
    pos = jnp.sum(jnp.where(onehot, offsets, 0), axis=-1) + rt[2:4].astype(jnp.int32)
    pos3 = pos.reshape(2, tiles, TOK_TILE).transpose(1, 0, 2).reshape(tiles, 1, 2 * TOK_TILE)

    n_sorted = n_sorted_tiles * MOE_TILE
    row = (d // LANES, LANES)
    pos_spec = pl.BlockSpec((1, 1, 2 * TOK_TILE), lambda i: (i, 0, 0), memory_space=pltpu.SMEM)
    xs_sorted = pl.pallas_call(
        functools.partial(_dispatch_kernel, **common),
        out_shape=jax.ShapeDtypeStruct((n_sorted,) + row, F32),
        grid=(tiles,),
        in_specs=[
            pos_spec,
            pl.BlockSpec((TOK_TILE, d), lambda i: (i, 0)),
            pl.BlockSpec(mod.shape, const),
            pl.BlockSpec((1, d), const),
            pl.BlockSpec(memory_space=pl.ANY),
        ],
        out_specs=pl.BlockSpec(memory_space=pl.ANY),
        scratch_shapes=[pltpu.VMEM((TOK_TILE,) + row, F32), pltpu.SemaphoreType.DMA(())],
        input_output_aliases={4: 0},
        compiler_params=_params("arbitrary"),
        name="moe_dispatch",
    )(pos3, xflat, mod, nw2, jnp.zeros((n_sorted,) + row, F32))

    de = wg.shape[-1]
    y_sorted = pl.pallas_call(
        _gmm_kernel,
        out_shape=jax.ShapeDtypeStruct((n_sorted,) + row, F32),
        grid_spec=pltpu.PrefetchScalarGridSpec(
            num_scalar_prefetch=2,
            grid=(n_sorted_tiles,),
            in_specs=[
                pl.BlockSpec((MOE_TILE,) + row, lambda i, te, nu: (jnp.minimum(i, jnp.maximum(nu[0] - 1, 0)), 0, 0)),
                pl.BlockSpec((None, d, de), lambda i, te, nu: (te[i], 0, 0)),
                pl.BlockSpec((None, d, de), lambda i, te, nu: (te[i], 0, 0)),
                pl.BlockSpec((None, de, d), lambda i, te, nu: (te[i], 0, 0)),
            ],
            out_specs=pl.BlockSpec((MOE_TILE,) + row, lambda i, te, nu: (i, 0, 0)),
            scratch_shapes=[pltpu.VMEM((d, de), BF16), pltpu.VMEM((d, de), BF16), pltpu.VMEM((de, d), BF16)],
        ),
        compiler_params=_params("arbitrary"),
        name="moe_gmm",
    )(tile_expert, n_used, xs_sorted, wg, wu, wd)

    final = norm_final is not None
    nf = (norm_final if final else jnp.ones((d,), F32)).reshape(1, d)
    return pl.pallas_call(
        functools.partial(_combine_kernel, final_norm=final, **common),
        out_shape=jax.ShapeDtypeStruct((n_tok, d), F32),
        grid=(tiles,),
        in_specs=[
            pos_spec,
            pl.BlockSpec((TOK_TILE, d), lambda i: (i, 0)),
            pl.BlockSpec((TOK_TILE, LANES), lambda i: (i, 0)),
            pl.BlockSpec(mod.shape, const),
            pl.BlockSpec((1, d), const),
            pl.BlockSpec(memory_space=pl.ANY),
        ],
        out_specs=pl.BlockSpec((TOK_TILE, d), lambda i: (i, 0)),
        scratch_shapes=[pltpu.VMEM((TOK_TILE,) + row, F32), pltpu.VMEM((TOK_TILE,) + row, F32),
                        pltpu.SemaphoreType.DMA(())],
        compiler_params=_params("arbitrary"),
        name="moe_combine",
    )(pos3, xflat, rtok, mod, nf, y_sorted)


def _dn_proj_kernel(x_ref, xp_ref, xn_ref, mod_ref, nw_ref, wqkv_ref, wz_ref, wabh_ref, wabl_ref, wconv_ref, gpar_ref,
                    tril_ref, q_ref, k_ref, v_ref, z_ref, gct_ref, gcr_ref, hf_ref,
                    *, d, tile, tiles_per_seq, lat, ctx_row):
    i = pl.program_id(0)
    row = (i // tiles_per_seq) if lat else ctx_row
    sh = mod_ref[pl.ds(row, 1), 0:d]
    sc = mod_ref[pl.ds(row, 1), d:2 * d]
    nw = nw_ref[...]
    first = (i % tiles_per_seq) == 0
    last = (i % tiles_per_seq) == tiles_per_seq - 1
    halo = SUBLANES
    hf_ref[0:halo] = jnp.where(first, 0.0, _norm_mod(xp_ref[...], nw, sc, sh))
    hm = _norm_mod(x_ref[...], nw, sc, sh)
    hf_ref[halo:halo + tile] = hm
    hf_ref[halo + tile:] = jnp.where(last, 0.0, _norm_mod(xn_ref[...], nw, sc, sh))
    hb = hf_ref[...].astype(BF16)

    z_ref[...] = _dot(hm.astype(BF16), wz_ref[...])

    ab = _dot3(hm, wabh_ref[...], wabl_ref[...])
    a_log, dt_bias, is_g, is_f, is_b = (gpar_ref[j:j + 1, :] for j in range(5))
    xg = ab + dt_bias
    softplus = jnp.maximum(xg, 0.0) + jnp.log(1.0 + jnp.exp(-jnp.abs(xg)))
    slab = jnp.where(is_g > 0.5, -jnp.exp(a_log) * softplus, jax.nn.sigmoid(ab))
    s_hi, s_lo = _split_bf16(slab)
    pre, tot = [], []
    for cc in range(tile // DN_CHUNK):
        rows = slice(cc * DN_CHUNK, (cc + 1) * DN_CHUNK)
        p = _dot(tril_ref[...], s_hi[rows]) + _dot(tril_ref[...], s_lo[rows])
        pre.append(p)
        tot.append(jnp.broadcast_to(p[DN_CHUNK - 1:DN_CHUNK], p.shape))
    gpre = jnp.concatenate(pre, axis=0)
    gsuf = jnp.concatenate(tot, axis=0) - gpre + slab
    gcs = jnp.where(is_f > 0.5, gpre, jnp.where(is_b > 0.5, gsuf, slab))
    gct_ref[...] = gcs
    gcs_t = gcs.T
    n_ab = 4 * DN_HEADS
    for cc in range(tile // DN_CHUNK):
        gcr_ref[cc] = gcs_t[0:n_ab, cc * DN_CHUNK:(cc + 1) * DN_CHUNK]

    dk = d // DN_HEADS
    cw = 4 * dk
    outs = (q_ref, k_ref, v_ref)
    for cc in range(3 * d // cw):
        cols = slice(cc * cw, (cc + 1) * cw)
        pc = _dot(hb, wqkv_ref[:, cols])
        wc = wconv_ref[:, cols]
        n_rows = tile + 2 * halo
        conv = (pltpu.roll(pc, 2, 0)[halo:halo + tile] * wc[0:1]
                + pltpu.roll(pc, 1, 0)[halo:halo + tile] * wc[1:2]
                + pc[halo:halo + tile] * wc[2:3]
                + pltpu.roll(pc, n_rows - 1, 0)[halo:halo + tile] * wc[3:4])
        act = _silu(conv)
        which = (cc * cw) // d
        for hh in range(cw // dk):
            head = ((cc * cw) % d) // dk + hh
            t = act[:, hh * dk:(hh + 1) * dk]
            if which < 2:
                t = t * lax.rsqrt(jnp.sum(t * t, axis=-1, keepdims=True) + EPS)
            if which == 0:
                t = t * (dk ** -0.5)
            outs[which][head] = t


def _dn_proj(xflat, tok_off, n_seq, seq_len, tile, lat, mod, nw, wqkv, wz, wab, wconv, gpar, ctx_row):
    d = xflat.shape[1]
    dk = d // DN_HEADS
    t_n = n_seq * seq_len
    tiles = t_n // tile
    tiles_per_seq = seq_len // tile
    boff = tok_off // tile
    hoff = tok_off // SUBLANES
    hper = tile // SUBLANES
    hmax = xflat.shape[0] // SUBLANES - 1
    tril = jnp.asarray(np.tril(np.ones((DN_CHUNK, DN_CHUNK), np.float32)), dtype=BF16)
    wabh, wabl = _split_bf16(wab)
    const = lambda i: (0, 0)
    hshape = jax.ShapeDtypeStruct((DN_HEADS, t_n, dk), F32)
    hspec = pl.BlockSpec((DN_HEADS, tile, dk), lambda i: (0, i, 0))
    n_ab = 4 * DN_HEADS
    return pl.pallas_call(
        functools.partial(_dn_proj_kernel, d=d, tile=tile, tiles_per_seq=tiles_per_seq, lat=lat, ctx_row=ctx_row),
        out_shape=(hshape, hshape, hshape,
                   jax.ShapeDtypeStruct((t_n, d), F32),
                   jax.ShapeDtypeStruct((t_n, LANES), F32),
                   jax.ShapeDtypeStruct((t_n // DN_CHUNK, n_ab, DN_CHUNK), F32)),
        grid=(tiles,),
        in_specs=[
            pl.BlockSpec((tile, d), lambda i: (boff + i, 0)),
            pl.BlockSpec((SUBLANES, d), lambda i: (jnp.maximum(hoff + i * hper - 1, 0), 0)),
            pl.BlockSpec((SUBLANES, d), lambda i: (jnp.minimum(hoff + (i + 1) * hper, hmax), 0)),
            pl.BlockSpec(mod.shape, const),
            pl.BlockSpec((1, d), const),
            pl.BlockSpec(wqkv.shape, const),
            pl.BlockSpec(wz.shape, const),
            pl.BlockSpec(wab.shape, const),
            pl.BlockSpec(wab.shape, const),
            pl.BlockSpec(wconv.shape, const),
            pl.BlockSpec(gpar.shape, const),
            pl.BlockSpec((DN_CHUNK, DN_CHUNK), const),
        ],
        out_specs=(hspec, hspec, hspec,
                   pl.BlockSpec((tile, d), lambda i: (i, 0)),
                   pl.BlockSpec((tile, LANES), lambda i: (i, 0)),
                   pl.BlockSpec((tile // DN_CHUNK, n_ab, DN_CHUNK), lambda i: (i, 0, 0))),
        scratch_shapes=[pltpu.VMEM((tile + 2 * SUBLANES, d), F32)],
        compiler_params=_params("arbitrary"),
        name="dn_proj_lat" if lat else "dn_proj_ctx",
    )(xflat, xflat, xflat, mod, nw.reshape(1, d), wqkv, wz, wabh, wabl, wconv, gpar, tril)


def _dn_chunk_kernel(q_ref, k_ref, v_ref, gct_ref, gcr_ref, s0_ref, *rest, blk, reverse, need_output):
    n_out = 2 if need_output else 1
    o_ref = rest[0] if need_output else None
    sf_ref, s_ref = rest[n_out - 1], rest[n_out]
    bufs = rest[n_out + 1:]
    sets = (bufs[:len(bufs) // 2], bufs[len(bufs) // 2:])
    j = pl.program_id(1)
    nchunk = blk // DN_CHUNK
    c = DN_CHUNK
    dv = v_ref.shape[-1]
    heads = range(DN_HEADS)

    @pl.when(j == 0)
    def _():
        s_ref[...] = s0_ref[...]

    ii = lax.broadcasted_iota(jnp.int32, (c, c), 0)
    jj = lax.broadcasted_iota(jnp.int32, (c, c), 1)
    strict = (ii < jj) if reverse else (ii > jj)
    incl = (ii <= jj) if reverse else (ii >= jj)
    field = 2 if reverse else 0
    nt_dims = (((1,), (1,)), ((), ()))

    def g_last(gcc):
        return gcc[0:1, :] if reverse else gcc[c - 1:c, :]

    def prep(pair, buf):
        u_s, wq_s, kdt_s = buf[:3]
        inst = []
        for cc in range(DN_PREP_CHUNKS):
            ci = pair * DN_PREP_CHUNKS + cc
            rows = slice(ci * c, (ci + 1) * c)
            gcols = gct_ref[rows, :]
            grows = gcr_ref[ci]
            for h in heads:
                ln = 4 * h + field
                inst.append((cc, rows, h, gcols[:, ln:ln + 1], gcols[:, ln + 1:ln + 2], grows[ln:ln + 1, :]))
        kc = [k_ref[h, rows, :] for _, rows, h, _, _, _ in inst]
        vc = [v_ref[h, rows, :] for _, rows, h, _, _, _ in inst]
        decay = [jnp.exp(jnp.where(incl, gcc - gcr, 0.0)) for _, _, _, gcc, _, gcr in inst]
        egc = [jnp.exp(t[3]) for t in inst]
        kb = [kc_i * t[4] for kc_i, t in zip(kc, inst)]
        if need_output:
            qc = [q_ref[h, rows, :] for _, rows, h, _, _, _ in inst]
            lhs = [jnp.concatenate([kb_i, qc_i], axis=0).astype(BF16) for kb_i, qc_i in zip(kb, qc)]
        else:
            lhs = [kb_i.astype(BF16) for kb_i in kb]
        kq = [lax.dot_general(l_i, kc_i.astype(BF16), nt_dims, preferred_element_type=F32)
              for l_i, kc_i in zip(lhs, kc)]
        pb = [jnp.where(strict, -(kq_i[0:c] * d_i), 0.0).astype(BF16) for kq_i, d_i in zip(kq, decay)]
        rhs = [jnp.concatenate([vc_i * t[4], kb_i * e_i], axis=1) for vc_i, kb_i, e_i, t in zip(vc, kb, egc, inst)]
        x = [r_i + _dot(p_i, r_i.astype(BF16)) for p_i, r_i in zip(pb, rhs)]
        for _ in range(5):
            pb = [_dot(p_i, p_i).astype(BF16) for p_i in pb]
            x = [x_i + _dot(p_i, x_i.astype(BF16)) for p_i, x_i in zip(pb, x)]
        for n_i, (cc, _, h, gcc, _, _) in enumerate(inst):
            loc = slice(cc * c, (cc + 1) * c)
            u_s[h, loc, :] = x[n_i][:, 0:dv]
            kdt_s[h, cc] = (kc[n_i] * jnp.exp(g_last(gcc) - gcc)).T.astype(BF16)
            if need_output:
                wq_s[h, cc] = jnp.concatenate([x[n_i][:, dv:], qc[n_i] * egc[n_i]], axis=0).astype(BF16)
                buf[3][h, loc, :] = jnp.where(incl, kq[n_i][c:2 * c] * decay[n_i], 0.0).astype(BF16)
            else:
                wq_s[h, cc] = x[n_i][:, dv:].astype(BF16)

    def recur(pair, cc, buf):
        u_s, wq_s, kdt_s = buf[:3]
        ci = pair * DN_PREP_CHUNKS + cc
        rows = slice(ci * c, (ci + 1) * c)
        loc = slice(cc * c, (cc + 1) * c)
        gcols = gct_ref[rows, :]
        s = [s_ref[h] for h in heads]
        sb = [s_h.astype(BF16) for s_h in s]
        ws = [_dot(wq_s[h, cc], sb[h]) for h in heads]
        vb = [(u_s[h, loc, :] - ws[h][0:c]).astype(BF16) for h in heads]
        if need_output:
            for h in heads:
                o_ref[h, rows, :] = ws[h][c:2 * c] + _dot(buf[3][h, loc, :], vb[h])
        for h in heads:
            eg = jnp.exp(g_last(gcols[:, 4 * h + field:4 * h + field + 1]))
            s_ref[h] = s[h] * eg + _dot(kdt_s[h, cc], vb[h])

    pairs = list(range(nchunk // DN_PREP_CHUNKS))
    within = list(range(DN_PREP_CHUNKS))
    if reverse:
        pairs.reverse()
        within.reverse()
    prep(pairs[0], sets[0])
    for n_p, pair in enumerate(pairs):
        for cc in within:
            recur(pair, cc, sets[n_p % 2])
        if n_p + 1 < len(pairs):
            prep(pairs[n_p + 1], sets[(n_p + 1) % 2])

    @pl.when(j == pl.num_programs(1) - 1)
    def _():
        sf_ref[...] = s_ref[...]


def _dn_scan(q, k, v, gct, gcr, s0, n_seq, seq_len, blk, reverse, need_output):
    dk = q.shape[-1]
    nblk = seq_len // blk
    order = (lambda j: nblk - 1 - j) if reverse else (lambda j: j)
    hspec = pl.BlockSpec((DN_HEADS, blk, dk), lambda b, j: (0, b * nblk + order(j), 0))
    sspec = pl.BlockSpec((None, DN_HEADS, dk, dk), lambda b, j: (b, 0, 0, 0))
    n_ab = gcr.shape[1]
    s_shape = jax.ShapeDtypeStruct((n_seq, DN_HEADS, dk, dk), F32)
    out_shape = [s_shape]
    out_specs = [sspec]
    wq_rows = 2 * DN_CHUNK if need_output else DN_CHUNK
    pair_rows = DN_PREP_CHUNKS * DN_CHUNK
    buf_set = [pltpu.VMEM((DN_HEADS, pair_rows, dk), F32),
               pltpu.VMEM((DN_HEADS, DN_PREP_CHUNKS, wq_rows, dk), BF16),
               pltpu.VMEM((DN_HEADS, DN_PREP_CHUNKS, dk, DN_CHUNK), BF16)]
    if need_output:
        out_shape.insert(0, jax.ShapeDtypeStruct(q.shape, F32))
        out_specs.insert(0, hspec)
        buf_set.append(pltpu.VMEM((DN_HEADS, pair_rows, DN_CHUNK), BF16))
    scratch = [pltpu.VMEM((DN_HEADS, dk, dk), F32)] + buf_set + buf_set
    res = pl.pallas_call(
        functools.partial(_dn_chunk_kernel, blk=blk, reverse=reverse, need_output=need_output),
        out_shape=tuple(out_shape),
        grid=(n_seq, nblk),
        in_specs=[
            hspec, hspec, hspec,
            pl.BlockSpec((blk, LANES), lambda b, j: (b * nblk + order(j), 0)),
            pl.BlockSpec((blk // DN_CHUNK, n_ab, DN_CHUNK), lambda b, j: (b * nblk + order(j), 0, 0)),
            sspec,
        ],
        out_specs=tuple(out_specs),
        scratch_shapes=scratch,
        compiler_params=_params("arbitrary", "arbitrary"),
        name="dn_scan_" + ("bwd" if reverse else "fwd") + ("_out" if need_output else "_state"),
    )(q, k, v, gct, gcr, s0)
    return (res[0], res[1]) if need_output else (None, res[0])


def _dn_out_kernel(of_ref, ob_ref, z_ref, x_ref, mod_ref, nrm_ref, wout_ref, o_ref, *, d, tiles_per_batch):
    i = pl.program_id(0)
    row = i // tiles_per_batch
    gt = mod_ref[pl.ds(row, 1), 2 * d:3 * d]
    dk = d // DN_HEADS
    parts = []
    for h in range(DN_HEADS):
        o = of_ref[h] + ob_ref[h]
        o = o * lax.rsqrt(jnp.mean(o * o, axis=-1, keepdims=True) + EPS) * nrm_ref[...]
        parts.append((o * _silu(z_ref[:, h * dk:(h + 1) * dk])).astype(BF16))
    y = _dot(jnp.concatenate(parts, axis=1), wout_ref[...])
    o_ref[...] = x_ref[...] + gt * y


def _deltanet_layer(xflat, bsz, n, c_len, mod, nw, w_in, w_conv, a_log, dt_bias, dn_norm, w_out):
    d = xflat.shape[1]
    dk = d // DN_HEADS
    t_lat = bsz * n
    ctx_row = bsz
    wqkv = w_in[:, :3 * d].astype(BF16)
    wz = w_in[:, 3 * d:4 * d].astype(BF16)
    n_ab = 4 * DN_HEADS
    wab = w_in[:, 4 * d:].reshape(d, 4, DN_HEADS).transpose(0, 2, 1).reshape(d, n_ab)
    wab = jnp.zeros((d, LANES), F32).at[:, :n_ab].set(wab)
    lane = np.arange(LANES)
    fld, head, used = lane % 4, np.minimum(lane // 4, DN_HEADS - 1), lane < n_ab
    is_g = used & (fld % 2 == 0)
    dirn = fld // 2
    gpar = jnp.zeros((SUBLANES, LANES), F32)
    gpar = gpar.at[0].set(jnp.where(is_g, a_log[dirn, head], 0.0))
    gpar = gpar.at[1].set(jnp.where(is_g, dt_bias[dirn, head], 0.0))
    gpar = gpar.at[2].set(jnp.asarray(is_g, F32))
    gpar = gpar.at[3].set(jnp.asarray(used & (fld == 0), F32))
    gpar = gpar.at[4].set(jnp.asarray(used & (fld == 2), F32))

    proj = functools.partial(_dn_proj, xflat, mod=mod, nw=nw, wqkv=wqkv, wz=wz, wab=wab, wconv=w_conv, gpar=gpar,
                             ctx_row=ctx_row)
    q, k, v, z, gct, gcr = proj(tok_off=0, n_seq=bsz, seq_len=n, tile=TOK_TILE, lat=True)
    qc, kc, vc, _, gctc, gcrc = proj(tok_off=t_lat, n_seq=bsz, seq_len=c_len, tile=c_len, lat=False)

    s0 = jnp.zeros((bsz, DN_HEADS, dk, dk), F32)
    o_dir = []
    for reverse in (False, True):
        _, s_ctx = _dn_scan(qc, kc, vc, gctc, gcrc, s0, bsz, c_len, c_len, reverse, False)
        o, _ = _dn_scan(q, k, v, gct, gcr, s_ctx, bsz, n, TOK_TILE, reverse, True)
        o_dir.append(o)

    tiles = t_lat // TOK_TILE
    const = lambda i: (0, 0)
    hspec = pl.BlockSpec((DN_HEADS, TOK_TILE, dk), lambda i: (0, i, 0))
    return pl.pallas_call(
        functools.partial(_dn_out_kernel, d=d, tiles_per_batch=n // TOK_TILE),
        out_shape=jax.ShapeDtypeStruct((t_lat, d), F32),
        grid=(tiles,),
        in_specs=[
            hspec, hspec,
            pl.BlockSpec((TOK_TILE, d), lambda i: (i, 0)),
            pl.BlockSpec((TOK_TILE, d), lambda i: (i, 0)),
            pl.BlockSpec(mod.shape, const),
            pl.BlockSpec((1, dk), const),
            pl.BlockSpec((d, d), const),
        ],
        out_specs=pl.BlockSpec((TOK_TILE, d), lambda i: (i, 0)),
        compiler_params=_params("arbitrary"),
        name="dn_out",
    )(o_dir[0], o_dir[1], z, xflat, mod, dn_norm.reshape(1, dk), w_out.astype(BF16))


def kernel(x, c, ctx, c_ctx, w_ada, b_ada, norm_mix, norm_ffn, w_pool, b_pool, pool_scale, w_dn_in, w_dn_conv, dn_a_log, dn_dt_bias, dn_norm, w_dn_out, w_rg, b_rg, w_re, b_re, w_e_gate, w_e_up, w_e_down, norm_final):
    bsz, n, d = x.shape
    c_len = ctx.shape[1]
    assert w_ada.shape[0] == 2 and bsz < COND_ROWS and n % POOL_TILE == 0 and c_len == 256
    assert (bsz * c_len) % TOK_TILE == 0 and d % (4 * LANES) == 0
    t_lat = bsz * n
    t_all = t_lat + bsz * c_len
    ctx_row = bsz
    cond = jnp.zeros((COND_ROWS, d), F32).at[:bsz].set(c.astype(F32)).at[ctx_row].set(c_ctx.astype(F32))
    mod = _ada(cond, w_ada, b_ada)

    xflat = _pool_layer(x.astype(F32), ctx.astype(F32), mod[0], norm_mix[0], w_pool[0], b_pool[0], pool_scale[0])
    xflat = _moe_layer(xflat, t_all, mod[0], norm_ffn[0], w_rg[0], b_rg[0], w_re[0], b_re[0],
                       w_e_gate[0], w_e_up[0], w_e_down[0], t_lat=t_lat, n=n, ctx_row=ctx_row, norm_final=None)

    xlat = _deltanet_layer(xflat, bsz, n, c_len, mod[1], norm_mix[1], w_dn_in[0], w_dn_conv[0], dn_a_log[0],
                           dn_dt_bias[0], dn_norm[0], w_dn_out[0])
    out = _moe_layer(xlat, t_lat, mod[1], norm_ffn[1], w_rg[1], b_rg[1], w_re[1], b_re[1],
                     w_e_gate[1], w_e_up[1], w_e_down[1], t_lat=t_lat, n=n, ctx_row=ctx_row, norm_final=norm_final)
    return out.reshape(bsz, n, d).astype(x.dtype)
```

```python
import functools

import numpy as np
import jax
import jax.numpy as jnp
from jax import lax
from jax.experimental import pallas as pl
from jax.experimental.pallas import tpu as pltpu

GRID_W = 64
POOL_WINDOWS = (2, 4, 8, 16)
DN_HEADS = 8
DN_CONV_W = 4
DN_CHUNK = 64
DN_PREP_CHUNKS = 2
N_GROUPS = 4
EXPERTS_PER_GROUP = 8
EPS = 1e-6

LANES = 128
SUBLANES = 8
VMEM_LIMIT_BYTES = 56 * 1024 * 1024

COND_ROWS = 8
TOK_TILE = 512
POOL_TILE = 1024
POOL_HALO = 512
MOE_TILE = 512
ROW_UNROLL = 8
GROUP = 8
BF16 = jnp.bfloat16
F32 = jnp.float32
HIGHEST = lax.Precision.HIGHEST


def _dot(a, b):
    return jnp.dot(a, b, preferred_element_type=F32)


def _dot_hi(a, b):
    return jnp.dot(a, b, precision=HIGHEST, preferred_element_type=F32)


def _split_bf16(a):
    hi = a.astype(BF16)
    lo = (a - hi.astype(F32)).astype(BF16)
    return hi, lo


def _dot3(a, b_hi, b_lo):
    a_hi, a_lo = _split_bf16(a)
    return _dot(a_hi, b_hi) + _dot(a_lo, b_hi) + _dot(a_hi, b_lo)


def _params(*sem):
    return pltpu.CompilerParams(dimension_semantics=sem, vmem_limit_bytes=VMEM_LIMIT_BYTES)


def _norm_mod(x, nw, sc, sh):
    return x * lax.rsqrt(jnp.mean(x * x, axis=-1, keepdims=True) + EPS) * nw * (1.0 + sc) + sh


def _silu(x):
    return x * jax.nn.sigmoid(x)


def _ada_kernel(cond_ref, w_ref, b_ref, o_ref):
    o_ref[...] = _dot_hi(_silu(cond_ref[...]), w_ref[...]) + b_ref[...]


def _ada(cond, w_ada, b_ada):
    depth, d, d6 = w_ada.shape
    tn = d6 // 4
    return pl.pallas_call(
        _ada_kernel,
        out_shape=jax.ShapeDtypeStruct((depth, COND_ROWS, d6), F32),
        grid=(depth, d6 // tn),
        in_specs=[
            pl.BlockSpec((COND_ROWS, d), lambda i, j: (0, 0)),
            pl.BlockSpec((None, d, tn), lambda i, j: (i, 0, j)),
            pl.BlockSpec((None, 1, tn), lambda i, j: (i, 0, j)),
        ],
        out_specs=pl.BlockSpec((None, COND_ROWS, tn), lambda i, j: (i, 0, j)),
        compiler_params=_params("arbitrary", "arbitrary"),
        name="ada",
    )(cond, w_ada, b_ada.reshape(depth, 1, d6))


def _pool_group_out(d, whi_ref, wlo_ref, g):
    dhi, dlo = _split_bf16(d)
    return _dot(dhi, whi_ref[g]) + _dot(dlo, whi_ref[g]) + _dot(dhi, wlo_ref[g])


def _pool_lat_kernel(x_ref, xp_ref, xn_ref, mod_ref, nw_ref, whi_ref, wlo_ref, bp_ref, ps_ref, cm_ref, zero_ref,
                     o_ref, h_ref, *, d, n_rows):
    del zero_ref
    b = pl.program_id(0)
    i = pl.program_id(1)
    nt = pl.num_programs(1)
    gc = d // len(POOL_WINDOWS)
    sh = mod_ref[pl.ds(b, 1), 0:d]
    sc = mod_ref[pl.ds(b, 1), d:2 * d]
    gt = mod_ref[pl.ds(b, 1), 2 * d:3 * d]
    nw = nw_ref[...]
    h_ref[0:POOL_HALO] = jnp.where(i > 0, _norm_mod(xp_ref[...], nw, sc, sh), 0.0)
    h_ref[POOL_HALO:POOL_HALO + POOL_TILE] = _norm_mod(x_ref[...], nw, sc, sh)
    h_ref[POOL_HALO + POOL_TILE:] = jnp.where(i < nt - 1, _norm_mod(xn_ref[...], nw, sc, sh), 0.0)

    tok = lax.broadcasted_iota(jnp.int32, (POOL_TILE, 1), 0)
    r = i * (POOL_TILE // GRID_W) + (tok >> 6)
    c = tok & (GRID_W - 1)
    blk = cm_ref.shape[-1]
    for g, win in enumerate(POOL_WINDOWS):
        half = win // 2
        cols = slice(g * gc, (g + 1) * gc)
        rs = None
        for o in range(-half, half):
            part = h_ref[pl.ds(POOL_HALO + o * GRID_W, POOL_TILE), cols]
            rs = part if rs is None else rs + part
        segs = []
        for s in range(POOL_TILE // blk):
            hi, lo = _split_bf16(rs[s * blk:(s + 1) * blk])
            segs.append(_dot(cm_ref[g], hi) + _dot(cm_ref[g], lo))
        box = jnp.concatenate(segs, axis=0)
        cnt = ((jnp.minimum(r + half, n_rows) - jnp.maximum(r - half, 0))
               * (jnp.minimum(c + half, GRID_W) - jnp.maximum(c - half, 0))).astype(F32)
        dd = box / cnt - h_ref[POOL_HALO:POOL_HALO + POOL_TILE, cols]
        y = _pool_group_out(dd, whi_ref, wlo_ref, g)
        o_ref[:, cols] = x_ref[:, cols] + gt[:, cols] * ((y + bp_ref[:, cols]) * ps_ref[:, cols])


def _pool_ctx_kernel(x_ref, mod_ref, nw_ref, whi_ref, wlo_ref, bp_ref, ps_ref, cm_ref, prev_ref, o_ref, *, d, ctx_row):
    del prev_ref
    gc = d // len(POOL_WINDOWS)
    length = x_ref.shape[0]
    sh = mod_ref[ctx_row:ctx_row + 1, 0:d]
    sc = mod_ref[ctx_row:ctx_row + 1, d:2 * d]
    gt = mod_ref[ctx_row:ctx_row + 1, 2 * d:3 * d]
    x = x_ref[...]
    h = _norm_mod(x, nw_ref[...], sc, sh)
    t = lax.broadcasted_iota(jnp.int32, (length, 1), 0)
    for g, win in enumerate(POOL_WINDOWS):
        half = win // 2
        cols = slice(g * gc, (g + 1) * gc)
        hg = h[:, cols]
        hi, lo = _split_bf16(hg)
        box = _dot(cm_ref[g], hi) + _dot(cm_ref[g], lo)
        cnt = (jnp.minimum(t + half, length) - jnp.maximum(t - half, 0)).astype(F32)
        dd = box / cnt - hg
        y = _pool_group_out(dd, whi_ref, wlo_ref, g)
        o_ref[:, cols] = x[:, cols] + gt[:, cols] * ((y + bp_ref[:, cols]) * ps_ref[:, cols])


def _band_matrices(n, period):
    t = np.arange(n)
    out = []
    for win in POOL_WINDOWS:
        half = win // 2
        diff = t[None, :] - t[:, None]
        same = (t[None, :] // period) == (t[:, None] // period)
        out.append(((diff >= -half) & (diff < half) & same).astype(np.float32))
    return jnp.asarray(np.stack(out), dtype=BF16)


def _pool_layer(x, ctx, mod, nw, w_pool, b_pool, pool_scale):
    bsz, n, d = x.shape
    c_len = ctx.shape[1]
    n_groups = len(POOL_WINDOWS)
    gc = d // n_groups
    t_lat, t_ctx = bsz * n, bsz * c_len
    whi, wlo = _split_bf16(w_pool)
    nw2, bp2, ps2 = nw.reshape(1, d), b_pool.reshape(1, d), pool_scale.reshape(1, d)
    nt = n // POOL_TILE
    nh = n // POOL_HALO
    ratio = POOL_TILE // POOL_HALO
    const2 = lambda b, i: (0, 0)
    const3 = lambda b, i: (0, 0, 0)
    lat = pl.pallas_call(
        functools.partial(_pool_lat_kernel, d=d, n_rows=n // GRID_W),
        out_shape=jax.ShapeDtypeStruct((t_lat + t_ctx, d), F32),
        grid=(bsz, nt),
        in_specs=[
            pl.BlockSpec((None, POOL_TILE, d), lambda b, i: (b, i, 0)),
            pl.BlockSpec((None, POOL_HALO, d), lambda b, i: (b, jnp.maximum(ratio * i - 1, 0), 0)),
            pl.BlockSpec((None, POOL_HALO, d), lambda b, i: (b, jnp.minimum(ratio * i + ratio, nh - 1), 0)),
            pl.BlockSpec(mod.shape, const2),
            pl.BlockSpec((1, d), const2),
            pl.BlockSpec((n_groups, gc, gc), const3),
            pl.BlockSpec((n_groups, gc, gc), const3),
            pl.BlockSpec((1, d), const2),
            pl.BlockSpec((1, d), const2),
            pl.BlockSpec((n_groups, 256, 256), const3),
            pl.BlockSpec(memory_space=pl.ANY),
        ],
        out_specs=pl.BlockSpec((POOL_TILE, d), lambda b, i: (b * nt + i, 0)),
        scratch_shapes=[pltpu.VMEM((POOL_TILE + 2 * POOL_HALO, d), F32)],
        input_output_aliases={10: 0},
        compiler_params=_params("arbitrary", "arbitrary"),
        name="pool_lat",
    )(x, x, x, mod, nw2, whi, wlo, bp2, ps2, _band_matrices(256, GRID_W), jnp.zeros((t_lat + t_ctx, d), F32))
    const1 = lambda b: (0, 0)
    const13 = lambda b: (0, 0, 0)
    return pl.pallas_call(
        functools.partial(_pool_ctx_kernel, d=d, ctx_row=bsz),
        out_shape=jax.ShapeDtypeStruct((t_lat + t_ctx, d), F32),
        grid=(bsz,),
        in_specs=[
            pl.BlockSpec((None, c_len, d), lambda b: (b, 0, 0)),
            pl.BlockSpec(mod.shape, const1),
            pl.BlockSpec((1, d), const1),
            pl.BlockSpec((n_groups, gc, gc), const13),
            pl.BlockSpec((n_groups, gc, gc), const13),
            pl.BlockSpec((1, d), const1),
            pl.BlockSpec((1, d), const1),
            pl.BlockSpec((n_groups, c_len, c_len), const13),
            pl.BlockSpec(memory_space=pl.ANY),
        ],
        out_specs=pl.BlockSpec((c_len, d), lambda b: (t_lat // c_len + b, 0)),
        input_output_aliases={8: 0},
        compiler_params=_params("arbitrary"),
        name="pool_ctx",
    )(ctx, mod, nw2, whi, wlo, bp2, ps2, _band_matrices(c_len, c_len), lat)


def _for_rows(n, fn):
    def body(step, carry):
        base = pl.multiple_of(step * ROW_UNROLL, ROW_UNROLL)
        for u in range(ROW_UNROLL):
            fn(base + u)
        return carry

    lax.fori_loop(0, n // ROW_UNROLL, body, 0)


def _for_count(n, fn):
    def body(j, carry):
        fn(j)
        return carry

    lax.fori_loop(0, n, body, 0)


def _group(ref, g):
    return ref.at[pl.ds(pl.multiple_of(g * GROUP, GROUP), GROUP)]


def _mod_row(tile, lat_tiles, tiles_per_batch, ctx_row):
    return jnp.where(tile < lat_tiles, tile // tiles_per_batch, ctx_row)


def _route_kernel(x_ref, mod_ref, nw_ref, wrh_ref, wrl_ref, br_ref, ls_ref, up_ref,
                  rtok_ref, rt_ref, ng_ref, cb_ref, cnt_ref, carry_ref, *, d, lat_tiles, tiles_per_batch, ctx_row):
    i = pl.program_id(0)
    row = _mod_row(i, lat_tiles, tiles_per_batch, ctx_row)
    sh = mod_ref[pl.ds(row, 1), 3 * d:4 * d]
    sc = mod_ref[pl.ds(row, 1), 4 * d:5 * d]
    h = _norm_mod(x_ref[...], nw_ref[...], sc, sh)
    logits = _dot3(h, wrh_ref[...], wrl_ref[...]) + br_ref[...]
    lane = lax.broadcasted_iota(jnp.int32, logits.shape, 1).astype(F32)
    neg = jnp.float32(-jnp.inf)
    big = jnp.float32(1e9)
    is_g = lane < N_GROUPS
    lg = jnp.where(is_g, logits, neg)
    gmax = jnp.max(lg, axis=1, keepdims=True)
    g_idx = jnp.min(jnp.where(lg == gmax, lane, big), axis=1, keepdims=True)
    pg_top = 1.0 / jnp.sum(jnp.where(is_g, jnp.exp(lg - gmax), 0.0), axis=1, keepdims=True)
    lo = N_GROUPS + EXPERTS_PER_GROUP * g_idx
    le = jnp.where((lane >= lo) & (lane < lo + EXPERTS_PER_GROUP), logits, neg)
    e1 = jnp.max(le, axis=1, keepdims=True)
    i1 = jnp.min(jnp.where(le == e1, lane, big), axis=1, keepdims=True)
    le2 = jnp.where(lane == i1, neg, le)
    e2 = jnp.max(le2, axis=1, keepdims=True)
    i2 = jnp.min(jnp.where(le2 == e2, lane, big), axis=1, keepdims=True)
    r21 = jnp.exp(e2 - e1)
    w1 = pg_top / (1.0 + r21)
    w2 = pg_top * r21 / (1.0 + r21)
    id1 = i1 - N_GROUPS
    id2 = i2 - N_GROUPS

    @pl.when(i == 0)
    def _():
        carry_ref[...] = jnp.zeros_like(carry_ref)

    oh1 = (lane == id1).astype(F32)
    oh2 = (lane == id2).astype(F32)
    cum1 = _dot(ls_ref[...], oh1.astype(BF16))
    cum2 = _dot(ls_ref[...], oh2.astype(BF16))
    tot1 = jnp.sum(oh1, axis=0, keepdims=True)
    tot2 = jnp.sum(oh2, axis=0, keepdims=True)
    ng = jnp.floor((tot1 + tot2 + (GROUP - 1.0)) * (1.0 / GROUP))
    ng8 = jnp.broadcast_to(ng, (SUBLANES, ng.shape[1])).astype(BF16)
    lstart = GROUP * _dot(ng8, up_ref[...])[0:1]
    lpos1 = jnp.sum(oh1 * (cum1 + lstart), axis=1, keepdims=True)
    lpos2 = jnp.sum(oh2 * (cum2 + tot1 + lstart), axis=1, keepdims=True)
    carry = carry_ref[...]
    ng_ref[0] = ng
    cb_ref[0] = carry
    carry = carry + ng
    carry_ref[...] = carry
    cnt_ref[...] = carry

    fields = (id1, id2, lpos1, lpos2, w1, w2)
    slab = jnp.zeros_like(logits)
    for k, f in enumerate(fields):
        slab = jnp.where(lane == k, f, slab)
    rtok_ref[...] = slab
    rt_ref[...] = slab.T[0:SUBLANES]


def _dispatch_kernel(zl_ref, tab_ref, x_ref, mod_ref, nw_ref, o_ref, h_ref, l_ref, z_ref, sem,
                     *, d, lat_tiles, tiles_per_batch, ctx_row, n_zero_max):
    i = pl.program_id(0)
    tab_n = tab_ref.shape[-1]

    @pl.when(i == 0)
    def _():
        l_ref[...] = jnp.zeros_like(l_ref)
        z_ref[...] = jnp.zeros_like(z_ref)

        def zcopy(z):
            return pltpu.make_async_copy(z_ref, _group(o_ref, zl_ref[z]), sem)

        _for_count(zl_ref[n_zero_max], lambda z: zcopy(z).start())
        _for_count(zl_ref[n_zero_max], lambda z: zcopy(z).wait())

    row = _mod_row(i, lat_tiles, tiles_per_batch, ctx_row)
    sh = mod_ref[pl.ds(row, 1), 3 * d:4 * d]
    sc = mod_ref[pl.ds(row, 1), 4 * d:5 * d]
    h_ref[...] = _norm_mod(x_ref[...], nw_ref[...], sc, sh).reshape(h_ref.shape)

    def place(r):
        v = h_ref[r]
        l_ref[tab_ref[0, 0, r]] = v
        l_ref[tab_ref[0, 0, TOK_TILE + r]] = v

    _for_rows(TOK_TILE, place)

    def gcopy(j):
        return pltpu.make_async_copy(_group(l_ref, j), _group(o_ref, tab_ref[0, 0, 2 * TOK_TILE + j]), sem)

    n_groups = tab_ref[0, 0, tab_n - 1]
    _for_count(n_groups, lambda j: gcopy(j).start())
    _for_count(n_groups, lambda j: gcopy(j).wait())


def _gmm_kernel(te_ref, nu_ref, x_ref, wg_ref, wu_ref, wd_ref, o_ref, wgb_ref, wub_ref, wdb_ref):
    i = pl.program_id(0)

    @pl.when((i == 0) | (te_ref[i] != te_ref[jnp.maximum(i - 1, 0)]))
    def _():
        wgb_ref[...] = wg_ref[...].astype(BF16)
        wub_ref[...] = wu_ref[...].astype(BF16)
        wdb_ref[...] = wd_ref[...].astype(BF16)

    @pl.when(i < nu_ref[0])
    def _():
        x = x_ref[...].reshape(x_ref.shape[0], wgb_ref.shape[0]).astype(BF16)
        hid = _silu(_dot(x, wgb_ref[...])) * _dot(x, wub_ref[...])
        o_ref[...] = _dot(hid.astype(BF16), wdb_ref[...]).reshape(o_ref.shape)

    @pl.when(i >= nu_ref[0])
    def _():
        o_ref[...] = jnp.zeros_like(o_ref)


def _combine_kernel(tab_ref, x_ref, rtok_ref, mod_ref, nf_ref, y_ref, o_ref, l_ref, g1_ref, g2_ref, sem,
                    *, d, lat_tiles, tiles_per_batch, ctx_row, final_norm):
    i = pl.program_id(0)
    tab_n = tab_ref.shape[-1]
    row = _mod_row(i, lat_tiles, tiles_per_batch, ctx_row)
    gt = mod_ref[pl.ds(row, 1), 5 * d:6 * d]

    def gcopy(j):
        return pltpu.make_async_copy(_group(y_ref, tab_ref[0, 0, 2 * TOK_TILE + j]), _group(l_ref, j), sem)

    n_groups = tab_ref[0, 0, tab_n - 1]
    _for_count(n_groups, lambda j: gcopy(j).start())
    _for_count(n_groups, lambda j: gcopy(j).wait())

    def pick(r):
        g1_ref[r] = l_ref[tab_ref[0, 0, r]]
        g2_ref[r] = l_ref[tab_ref[0, 0, TOK_TILE + r]]

    _for_rows(TOK_TILE, pick)
    rt = rtok_ref[...]
    w1 = rt[:, 4:5]
    w2 = rt[:, 5:6]
    out = x_ref[...] + gt * (w1 * g1_ref[...].reshape(x_ref.shape) + w2 * g2_ref[...].reshape(x_ref.shape))
    if final_norm:
        out = out * lax.rsqrt(jnp.mean(out * out, axis=-1, keepdims=True) + EPS) * nf_ref[...]
    o_ref[...] = out


def _moe_layer(xflat, n_tok, mod, nw, w_rg, b_rg, w_re, b_re, wg, wu, wd, layer, *, t_lat, n, ctx_row, norm_final):
    d = xflat.shape[1]
    n_exp = w_re.shape[1]
    tiles = n_tok // TOK_TILE
    lat_tiles = t_lat // TOK_TILE
    tiles_per_batch = n // TOK_TILE
    common = dict(d=d, lat_tiles=lat_tiles, tiles_per_batch=tiles_per_batch, ctx_row=ctx_row)
    nw2 = nw.reshape(1, d)
    wr = jnp.zeros((d, LANES), F32).at[:, :N_GROUPS].set(w_rg).at[:, N_GROUPS:N_GROUPS + n_exp].set(w_re)
    br = jnp.zeros((1, LANES), F32).at[0, :N_GROUPS].set(b_rg).at[0, N_GROUPS:N_GROUPS + n_exp].set(b_re)
    wrh, wrl = _split_bf16(wr)
    lstrict = jnp.asarray(np.tril(np.ones((TOK_TILE, TOK_TILE), np.float32), -1), dtype=BF16)
    upper = jnp.asarray(np.triu(np.ones((LANES, LANES), np.float32), 1), dtype=BF16)
    const = lambda i: (0, 0)
    per_tile = pl.BlockSpec((1, 1, LANES), lambda i: (i, 0, 0))
    rtok, rt, ngf, cbf, cnt = pl.pallas_call(
        functools.partial(_route_kernel, **common),
        out_shape=(jax.ShapeDtypeStruct((n_tok, LANES), F32),
                   jax.ShapeDtypeStruct((SUBLANES, n_tok), F32),
                   jax.ShapeDtypeStruct((tiles, 1, LANES), F32),
                   jax.ShapeDtypeStruct((tiles, 1, LANES), F32),
                   jax.ShapeDtypeStruct((1, LANES), F32)),
        grid=(tiles,),
        in_specs=[
            pl.BlockSpec((TOK_TILE, d), lambda i: (i, 0)),
            pl.BlockSpec(mod.shape, const),
            pl.BlockSpec((1, d), const),
            pl.BlockSpec((d, LANES), const),
            pl.BlockSpec((d, LANES), const),
            pl.BlockSpec((1, LANES), const),
            pl.BlockSpec((TOK_TILE, TOK_TILE), const),
            pl.BlockSpec((LANES, LANES), const),
        ],
        out_specs=(pl.BlockSpec((TOK_TILE, LANES), lambda i: (i, 0)),
                   pl.BlockSpec((SUBLANES, TOK_TILE), lambda i: (0, i)),
                   per_tile, per_tile,
                   pl.BlockSpec((1, LANES), const)),
        scratch_shapes=[pltpu.VMEM((1, LANES), F32)],
        compiler_params=_params("arbitrary"),
        name="moe_route",
    )(xflat, mod, nw2, wrh, wrl, br, lstrict, upper)

    i32 = jnp.int32
    experts = jnp.arange(n_exp, dtype=i32)
    ng_t = ngf[:, 0, :n_exp].astype(i32)
    cb_t = cbf[:, 0, :n_exp].astype(i32)
    tot_g = cnt[0, :n_exp].astype(i32)
    gpt = MOE_TILE // GROUP
    local_groups = (2 * TOK_TILE) // GROUP + n_exp
    n_sorted_tiles = pl.cdiv((2 * n_tok) // GROUP + n_exp * tiles, gpt) + n_exp
    n_sorted_groups = n_sorted_tiles * gpt
    padded_g = ((tot_g + gpt - 1) // gpt) * gpt
    ends_g = jnp.cumsum(padded_g)
    off_g = ends_g - padded_g
    n_used = (ends_g[-1] // gpt).astype(i32).reshape(1)
    tile_ids = jnp.arange(n_sorted_tiles, dtype=i32)
    tile_expert = jnp.sum(tile_ids[:, None] * gpt >= ends_g[None, :], axis=1).astype(i32)
    last_expert = jnp.sum(jnp.maximum(n_used - 1, 0) * gpt >= ends_g).astype(i32)
    tile_expert = jnp.minimum(tile_expert, last_expert)
    lend = jnp.cumsum(ng_t, axis=1)
    jg = jnp.arange(local_groups, dtype=i32)
    seg = jnp.sum(jg[None, :, None] >= lend[:, None, :], axis=-1).astype(i32)
    base = off_g[None, :] + cb_t - (lend - ng_t)
    dst = jnp.sum(jnp.where(seg[:, :, None] == experts, base[:, None, :], 0), axis=-1) + jg[None, :]
    dst = jnp.where(seg < n_exp, dst, 0)
    lpos = rt[2:4].astype(i32).reshape(2, tiles, TOK_TILE).transpose(1, 0, 2).reshape(tiles, 2 * TOK_TILE)
    tab_n = 2 * TOK_TILE + pl.cdiv(local_groups + 1, LANES) * LANES
    fill = jnp.zeros((tiles, tab_n - 2 * TOK_TILE - local_groups - 1), i32)
    tab = jnp.concatenate([lpos, dst.astype(i32), fill, lend[:, -1:]], axis=1).reshape(tiles, 1, tab_n)
    n_zero_max = n_sorted_groups - (2 * n_tok) // GROUP
    zc = jnp.concatenate([padded_g - tot_g, n_sorted_groups - ends_g[-1:]])
    zs = jnp.concatenate([off_g + tot_g, ends_g[-1:]])
    zend = jnp.cumsum(zc)
    zi = jnp.arange(n_zero_max, dtype=i32)
    zseg = jnp.sum(zi[:, None] >= zend[None, :], axis=-1).astype(i32)
    zl = jnp.sum(jnp.where(zseg[:, None] == jnp.arange(n_exp + 1, dtype=i32), (zs - (zend - zc))[None, :], 0), axis=-1) + zi
    zlist = jnp.concatenate([jnp.where(zi < zend[-1], zl, 0), zend[-1:]]).astype(i32)

    n_sorted = n_sorted_groups * GROUP
    row = (d // LANES, LANES)
    local_rows = local_groups * GROUP
    xs_sorted = pl.pallas_call(
        functools.partial(_dispatch_kernel, n_zero_max=n_zero_max, **common),
        out_shape=jax.ShapeDtypeStruct((n_sorted,) + row, F32),
        grid_spec=pltpu.PrefetchScalarGridSpec(
            num_scalar_prefetch=1,
            grid=(tiles,),
            in_specs=[
                pl.BlockSpec((1, 1, tab_n), lambda i, zl: (i, 0, 0), memory_space=pltpu.SMEM),
                pl.BlockSpec((TOK_TILE, d), lambda i, zl: (i, 0)),
                pl.BlockSpec(mod.shape, lambda i, zl: (0, 0)),
                pl.BlockSpec((1, d), lambda i, zl: (0, 0)),
            ],
            out_specs=pl.BlockSpec(memory_space=pl.ANY),
            scratch_shapes=[pltpu.VMEM((TOK_TILE,) + row, F32), pltpu.VMEM((local_rows,) + row, F32),
                            pltpu.VMEM((GROUP,) + row, F32), pltpu.SemaphoreType.DMA(())],
        ),
        compiler_params=_params("arbitrary"),
        name="moe_dispatch",
    )(zlist, tab, xflat, mod, nw2)

    de = wg.shape[-1]
    y_sorted = pl.pallas_call(
        _gmm_kernel,
        out_shape=jax.ShapeDtypeStruct((n_sorted,) + row, F32),
        grid_spec=pltpu.PrefetchScalarGridSpec(
            num_scalar_prefetch=2,
            grid=(n_sorted_tiles,),
            in_specs=[
                pl.BlockSpec((MOE_TILE,) + row, lambda i, te, nu: (jnp.minimum(i, jnp.maximum(nu[0] - 1, 0)), 0, 0)),
                pl.BlockSpec((None, None, d, de), lambda i, te, nu: (layer, te[i], 0, 0)),
                pl.BlockSpec((None, None, d, de), lambda i, te, nu: (layer, te[i], 0, 0)),
                pl.BlockSpec((None, None, de, d), lambda i, te, nu: (layer, te[i], 0, 0)),
            ],
            out_specs=pl.BlockSpec((MOE_TILE,) + row, lambda i, te, nu: (i, 0, 0)),
            scratch_shapes=[pltpu.VMEM((d, de), BF16), pltpu.VMEM((d, de), BF16), pltpu.VMEM((de, d), BF16)],
        ),
        compiler_params=_params("arbitrary"),
        name="moe_gmm",
    )(tile_expert, n_used, xs_sorted, wg, wu, wd)

    final = norm_final is not None
    nf = (norm_final if final else jnp.ones((d,), F32)).reshape(1, d)
    return pl.pallas_call(
        functools.partial(_combine_kernel, final_norm=final, **common),
        out_shape=jax.ShapeDtypeStruct((n_tok, d), F32),
        grid=(tiles,),
        in_specs=[
            pl.BlockSpec((1, 1, tab_n), lambda i: (i, 0, 0), memory_space=pltpu.SMEM),
            pl.BlockSpec((TOK_TILE, d), lambda i: (i, 0)),
            pl.BlockSpec((TOK_TILE, LANES), lambda i: (i, 0)),
            pl.BlockSpec(mod.shape, const),
            pl.BlockSpec((1, d), const),
            pl.BlockSpec(memory_space=pl.ANY),
        ],
        out_specs=pl.BlockSpec((TOK_TILE, d), lambda i: (i, 0)),
        scratch_shapes=[pltpu.VMEM((local_rows,) + row, F32), pltpu.VMEM((TOK_TILE,) + row, F32),
                        pltpu.VMEM((TOK_TILE,) + row, F32), pltpu.SemaphoreType.DMA(())],
        compiler_params=_params("arbitrary"),
        name="moe_combine",
    )(tab, xflat, rtok, mod, nf, y_sorted)


def _dn_proj_kernel(x_ref, xp_ref, xn_ref, mod_ref, nw_ref, wqkv_ref, wz_ref, wabh_ref, wabl_ref, wconv_ref, gpar_ref,
                    tril_ref, q_ref, k_ref, v_ref, z_ref, gct_ref, gcr_ref, hf_ref,
                    *, d, tile, tiles_per_seq, lat, ctx_row):
    i = pl.program_id(0)
    row = (i // tiles_per_seq) if lat else ctx_row
    sh = mod_ref[pl.ds(row, 1), 0:d]
    sc = mod_ref[pl.ds(row, 1), d:2 * d]
    nw = nw_ref[...]
    first = (i % tiles_per_seq) == 0
    last = (i % tiles_per_seq) == tiles_per_seq - 1
    halo = SUBLANES
    hf_ref[0:halo] = jnp.where(first, 0.0, _norm_mod(xp_ref[...], nw, sc, sh))
    hm = _norm_mod(x_ref[...], nw, sc, sh)
    hf_ref[halo:halo + tile] = hm
    hf_ref[halo + tile:] = jnp.where(last, 0.0, _norm_mod(xn_ref[...], nw, sc, sh))
    hb = hf_ref[...].astype(BF16)

    z_ref[...] = _dot(hm.astype(BF16), wz_ref[...])

    ab = _dot3(hm, wabh_ref[...], wabl_ref[...])
    a_log, dt_bias, is_g, is_f, is_b = (gpar_ref[j:j + 1, :] for j in range(5))
    xg = ab + dt_bias
    softplus = jnp.maximum(xg, 0.0) + jnp.log(1.0 + jnp.exp(-jnp.abs(xg)))
    slab = jnp.where(is_g > 0.5, -jnp.exp(a_log) * softplus, jax.nn.sigmoid(ab))
    s_hi, s_lo = _split_bf16(slab)
    pre, tot = [], []
    for cc in range(tile // DN_CHUNK):
        rows = slice(cc * DN_CHUNK, (cc + 1) * DN_CHUNK)
        p = _dot(tril_ref[...], s_hi[rows]) + _dot(tril_ref[...], s_lo[rows])
        pre.append(p)
        tot.append(jnp.broadcast_to(p[DN_CHUNK - 1:DN_CHUNK], p.shape))
    gpre = jnp.concatenate(pre, axis=0)
    gsuf = jnp.concatenate(tot, axis=0) - gpre + slab
    gcs = jnp.where(is_f > 0.5, gpre, jnp.where(is_b > 0.5, gsuf, slab))
    gct_ref[...] = gcs
    gcs_t = gcs.T
    n_ab = 4 * DN_HEADS
    for cc in range(tile // DN_CHUNK):
        gcr_ref[cc] = gcs_t[0:n_ab, cc * DN_CHUNK:(cc + 1) * DN_CHUNK]

    dk = d // DN_HEADS
    cw = 4 * dk
    outs = (q_ref, k_ref, v_ref)
    for cc in range(3 * d // cw):
        cols = slice(cc * cw, (cc + 1) * cw)
        pc = _dot(hb, wqkv_ref[:, cols])
        wc = wconv_ref[:, cols]
        n_rows = tile + 2 * halo
        conv = (pltpu.roll(pc, 2, 0)[halo:halo + tile] * wc[0:1]
                + pltpu.roll(pc, 1, 0)[halo:halo + tile] * wc[1:2]
                + pc[halo:halo + tile] * wc[2:3]
                + pltpu.roll(pc, n_rows - 1, 0)[halo:halo + tile] * wc[3:4])
        act = _silu(conv)
        which = (cc * cw) // d
        for hh in range(cw // dk):
            head = ((cc * cw) % d) // dk + hh
            t = act[:, hh * dk:(hh + 1) * dk]
            if which < 2:
                t = t * lax.rsqrt(jnp.sum(t * t, axis=-1, keepdims=True) + EPS)
            if which == 0:
                t = t * (dk ** -0.5)
            outs[which][head] = t


def _dn_proj(xflat, tok_off, n_seq, seq_len, tile, lat, mod, nw, wqkv, wz, wab, wconv, gpar, ctx_row):
    d = xflat.shape[1]
    dk = d // DN_HEADS
    t_n = n_seq * seq_len
    tiles = t_n // tile
    tiles_per_seq = seq_len // tile
    boff = tok_off // tile
    hoff = tok_off // SUBLANES
    hper = tile // SUBLANES
    hmax = xflat.shape[0] // SUBLANES - 1
    tril = jnp.asarray(np.tril(np.ones((DN_CHUNK, DN_CHUNK), np.float32)), dtype=BF16)
    wabh, wabl = _split_bf16(wab)
    const = lambda i: (0, 0)
    hshape = jax.ShapeDtypeStruct((DN_HEADS, t_n, dk), F32)
    hspec = pl.BlockSpec((DN_HEADS, tile, dk), lambda i: (0, i, 0))
    n_ab = 4 * DN_HEADS
    return pl.pallas_call(
        functools.partial(_dn_proj_kernel, d=d, tile=tile, tiles_per_seq=tiles_per_seq, lat=lat, ctx_row=ctx_row),
        out_shape=(hshape, hshape, hshape,
                   jax.ShapeDtypeStruct((t_n, d), F32),
                   jax.ShapeDtypeStruct((t_n, LANES), F32),
                   jax.ShapeDtypeStruct((t_n // DN_CHUNK, n_ab, DN_CHUNK), F32)),
        grid=(tiles,),
        in_specs=[
            pl.BlockSpec((tile, d), lambda i: (boff + i, 0)),
            pl.BlockSpec((SUBLANES, d), lambda i: (jnp.maximum(hoff + i * hper - 1, 0), 0)),
            pl.BlockSpec((SUBLANES, d), lambda i: (jnp.minimum(hoff + (i + 1) * hper, hmax), 0)),
            pl.BlockSpec(mod.shape, const),
            pl.BlockSpec((1, d), const),
            pl.BlockSpec(wqkv.shape, const),
            pl.BlockSpec(wz.shape, const),
            pl.BlockSpec(wab.shape, const),
            pl.BlockSpec(wab.shape, const),
            pl.BlockSpec(wconv.shape, const),
            pl.BlockSpec(gpar.shape, const),
            pl.BlockSpec((DN_CHUNK, DN_CHUNK), const),
        ],
        out_specs=(hspec, hspec, hspec,
                   pl.BlockSpec((tile, d), lambda i: (i, 0)),
                   pl.BlockSpec((tile, LANES), lambda i: (i, 0)),
                   pl.BlockSpec((tile // DN_CHUNK, n_ab, DN_CHUNK), lambda i: (i, 0, 0))),
        scratch_shapes=[pltpu.VMEM((tile + 2 * SUBLANES, d), F32)],
        compiler_params=_params("arbitrary"),
        name="dn_proj_lat" if lat else "dn_proj_ctx",
    )(xflat, xflat, xflat, mod, nw.reshape(1, d), wqkv, wz, wabh, wabl, wconv, gpar, tril)


def _dn_chunk_kernel(q_ref, k_ref, v_ref, gct_ref, gcr_ref, s0_ref, *rest, blk, reverse, need_output):
    n_out = 2 if need_output else 1
    o_ref = rest[0] if need_output else None
    sf_ref, s_ref = rest[n_out - 1], rest[n_out]
    bufs = rest[n_out + 1:]
    sets = (bufs[:len(bufs) // 2], bufs[len(bufs) // 2:])
    j = pl.program_id(1)
    nchunk = blk // DN_CHUNK
    c = DN_CHUNK
    dv = v_ref.shape[-1]
    heads = range(DN_HEADS)

    @pl.when(j == 0)
    def _():
        s_ref[...] = s0_ref[...]

    ii = lax.broadcasted_iota(jnp.int32, (c, c), 0)
    jj = lax.broadcasted_iota(jnp.int32, (c, c), 1)
    strict = (ii < jj) if reverse else (ii > jj)
    incl = (ii <= jj) if reverse else (ii >= jj)
    field = 2 if reverse else 0
    nt_dims = (((1,), (1,)), ((), ()))

    def g_last(gcc):
        return gcc[0:1, :] if reverse else gcc[c - 1:c, :]

    def prep(pair, buf):
        u_s, wq_s, kdt_s = buf[:3]
        inst = []
        for cc in range(DN_PREP_CHUNKS):
            ci = pair * DN_PREP_CHUNKS + cc
            rows = slice(ci * c, (ci + 1) * c)
            gcols = gct_ref[rows, :]
            grows = gcr_ref[ci]
            for h in heads:
                ln = 4 * h + field
                inst.append((cc, rows, h, gcols[:, ln:ln + 1], gcols[:, ln + 1:ln + 2], grows[ln:ln + 1, :]))
        kc = [k_ref[h, rows, :] for _, rows, h, _, _, _ in inst]
        vc = [v_ref[h, rows, :] for _, rows, h, _, _, _ in inst]
        decay = [jnp.exp(jnp.where(incl, gcc - gcr, 0.0)) for _, _, _, gcc, _, gcr in inst]
        egc = [jnp.exp(t[3]) for t in inst]
        kb = [kc_i * t[4] for kc_i, t in zip(kc, inst)]
        if need_output:
            qc = [q_ref[h, rows, :] for _, rows, h, _, _, _ in inst]
            lhs = [jnp.concatenate([kb_i, qc_i], axis=0).astype(BF16) for kb_i, qc_i in zip(kb, qc)]
        else:
            lhs = [kb_i.astype(BF16) for kb_i in kb]
        kq = [lax.dot_general(l_i, kc_i.astype(BF16), nt_dims, preferred_element_type=F32)
              for l_i, kc_i in zip(lhs, kc)]
        pb = [jnp.where(strict, -(kq_i[0:c] * d_i), 0.0).astype(BF16) for kq_i, d_i in zip(kq, decay)]
        rhs = [jnp.concatenate([vc_i * t[4], kb_i * e_i], axis=1) for vc_i, kb_i, e_i, t in zip(vc, kb, egc, inst)]
        x = [r_i + _dot(p_i, r_i.astype(BF16)) for p_i, r_i in zip(pb, rhs)]
        for _ in range(5):
            pb = [_dot(p_i, p_i).astype(BF16) for p_i in pb]
            x = [x_i + _dot(p_i, x_i.astype(BF16)) for p_i, x_i in zip(pb, x)]
        for n_i, (cc, _, h, gcc, _, _) in enumerate(inst):
            loc = slice(cc * c, (cc + 1) * c)
            u_s[h, loc, :] = x[n_i][:, 0:dv]
            kdt_s[h, cc] = (kc[n_i] * jnp.exp(g_last(gcc) - gcc)).T.astype(BF16)
            if need_output:
                wq_s[h, cc] = jnp.concatenate([x[n_i][:, dv:], qc[n_i] * egc[n_i]], axis=0).astype(BF16)
                buf[3][h, loc, :] = jnp.where(incl, kq[n_i][c:2 * c] * decay[n_i], 0.0).astype(BF16)
            else:
                wq_s[h, cc] = x[n_i][:, dv:].astype(BF16)

    def recur(pair, cc, buf):
        u_s, wq_s, kdt_s = buf[:3]
        ci = pair * DN_PREP_CHUNKS + cc
        rows = slice(ci * c, (ci + 1) * c)
        loc = slice(cc * c, (cc + 1) * c)
        gcols = gct_ref[rows, :]
        s = [s_ref[h] for h in heads]
        sb = [s_h.astype(BF16) for s_h in s]
        ws = [_dot(wq_s[h, cc], sb[h]) for h in heads]
        vb = [(u_s[h, loc, :] - ws[h][0:c]).astype(BF16) for h in heads]
        if need_output:
            for h in heads:
                o_ref[h, rows, :] = ws[h][c:2 * c] + _dot(buf[3][h, loc, :], vb[h])
        for h in heads:
            eg = jnp.exp(g_last(gcols[:, 4 * h + field:4 * h + field + 1]))
            s_ref[h] = s[h] * eg + _dot(kdt_s[h, cc], vb[h])

    pairs = list(range(nchunk // DN_PREP_CHUNKS))
    within = list(range(DN_PREP_CHUNKS))
    if reverse:
        pairs.reverse()
        within.reverse()
    prep(pairs[0], sets[0])
    for n_p, pair in enumerate(pairs):
        for cc in within:
            recur(pair, cc, sets[n_p % 2])
        if n_p + 1 < len(pairs):
            prep(pairs[n_p + 1], sets[(n_p + 1) % 2])

    @pl.when(j == pl.num_programs(1) - 1)
    def _():
        sf_ref[...] = s_ref[...]


def _dn_scan(q, k, v, gct, gcr, s0, n_seq, seq_len, blk, reverse, need_output):
    dk = q.shape[-1]
    nblk = seq_len // blk
    order = (lambda j: nblk - 1 - j) if reverse else (lambda j: j)
    hspec = pl.BlockSpec((DN_HEADS, blk, dk), lambda b, j: (0, b * nblk + order(j), 0))
    sspec = pl.BlockSpec((None, DN_HEADS, dk, dk), lambda b, j: (b, 0, 0, 0))
    n_ab = gcr.shape[1]
    s_shape = jax.ShapeDtypeStruct((n_seq, DN_HEADS, dk, dk), F32)
    out_shape = [s_shape]
    out_specs = [sspec]
    wq_rows = 2 * DN_CHUNK if need_output else DN_CHUNK
    pair_rows = DN_PREP_CHUNKS * DN_CHUNK
    buf_set = [pltpu.VMEM((DN_HEADS, pair_rows, dk), F32),
               pltpu.VMEM((DN_HEADS, DN_PREP_CHUNKS, wq_rows, dk), BF16),
               pltpu.VMEM((DN_HEADS, DN_PREP_CHUNKS, dk, DN_CHUNK), BF16)]
    if need_output:
        out_shape.insert(0, jax.ShapeDtypeStruct(q.shape, F32))
        out_specs.insert(0, hspec)
        buf_set.append(pltpu.VMEM((DN_HEADS, pair_rows, DN_CHUNK), BF16))
    scratch = [pltpu.VMEM((DN_HEADS, dk, dk), F32)] + buf_set + buf_set
    res = pl.pallas_call(
        functools.partial(_dn_chunk_kernel, blk=blk, reverse=reverse, need_output=need_output),
        out_shape=tuple(out_shape),
        grid=(n_seq, nblk),
        in_specs=[
            hspec, hspec, hspec,
            pl.BlockSpec((blk, LANES), lambda b, j: (b * nblk + order(j), 0)),
            pl.BlockSpec((blk // DN_CHUNK, n_ab, DN_CHUNK), lambda b, j: (b * nblk + order(j), 0, 0)),
            sspec,
        ],
        out_specs=tuple(out_specs),
        scratch_shapes=scratch,
        compiler_params=_params("arbitrary", "arbitrary"),
        name="dn_scan_" + ("bwd" if reverse else "fwd") + ("_out" if need_output else "_state"),
    )(q, k, v, gct, gcr, s0)
    return (res[0], res[1]) if need_output else (None, res[0])


def _dn_out_kernel(of_ref, ob_ref, z_ref, x_ref, mod_ref, nrm_ref, wout_ref, o_ref, *, d, tiles_per_batch):
    i = pl.program_id(0)
    row = i // tiles_per_batch
    gt = mod_ref[pl.ds(row, 1), 2 * d:3 * d]
    dk = d // DN_HEADS
    parts = []
    for h in range(DN_HEADS):
        o = of_ref[h] + ob_ref[h]
        o = o * lax.rsqrt(jnp.mean(o * o, axis=-1, keepdims=True) + EPS) * nrm_ref[...]
        parts.append((o * _silu(z_ref[:, h * dk:(h + 1) * dk])).astype(BF16))
    y = _dot(jnp.concatenate(parts, axis=1), wout_ref[...])
    o_ref[...] = x_ref[...] + gt * y


def _deltanet_layer(xflat, bsz, n, c_len, mod, nw, w_in, w_conv, a_log, dt_bias, dn_norm, w_out):
    d = xflat.shape[1]
    dk = d // DN_HEADS
    t_lat = bsz * n
    ctx_row = bsz
    wqkv = w_in[:, :3 * d].astype(BF16)
    wz = w_in[:, 3 * d:4 * d].astype(BF16)
    n_ab = 4 * DN_HEADS
    wab = w_in[:, 4 * d:].reshape(d, 4, DN_HEADS).transpose(0, 2, 1).reshape(d, n_ab)
    wab = jnp.zeros((d, LANES), F32).at[:, :n_ab].set(wab)
    lane = np.arange(LANES)
    fld, head, used = lane % 4, np.minimum(lane // 4, DN_HEADS - 1), lane < n_ab
    is_g = used & (fld % 2 == 0)
    dirn = fld // 2
    gpar = jnp.zeros((SUBLANES, LANES), F32)
    gpar = gpar.at[0].set(jnp.where(is_g, a_log[dirn, head], 0.0))
    gpar = gpar.at[1].set(jnp.where(is_g, dt_bias[dirn, head], 0.0))
    gpar = gpar.at[2].set(jnp.asarray(is_g, F32))
    gpar = gpar.at[3].set(jnp.asarray(used & (fld == 0), F32))
    gpar = gpar.at[4].set(jnp.asarray(used & (fld == 2), F32))

    proj = functools.partial(_dn_proj, xflat, mod=mod, nw=nw, wqkv=wqkv, wz=wz, wab=wab, wconv=w_conv, gpar=gpar,
                             ctx_row=ctx_row)
    q, k, v, z, gct, gcr = proj(tok_off=0, n_seq=bsz, seq_len=n, tile=TOK_TILE, lat=True)
    qc, kc, vc, _, gctc, gcrc = proj(tok_off=t_lat, n_seq=bsz, seq_len=c_len, tile=c_len, lat=False)

    s0 = jnp.zeros((bsz, DN_HEADS, dk, dk), F32)
    o_dir = []
    for reverse in (False, True):
        _, s_ctx = _dn_scan(qc, kc, vc, gctc, gcrc, s0, bsz, c_len, c_len, reverse, False)
        o, _ = _dn_scan(q, k, v, gct, gcr, s_ctx, bsz, n, TOK_TILE, reverse, True)
        o_dir.append(o)

    tiles = t_lat // TOK_TILE
    const = lambda i: (0, 0)
    hspec = pl.BlockSpec((DN_HEADS, TOK_TILE, dk), lambda i: (0, i, 0))
    return pl.pallas_call(
        functools.partial(_dn_out_kernel, d=d, tiles_per_batch=n // TOK_TILE),
        out_shape=jax.ShapeDtypeStruct((t_lat, d), F32),
        grid=(tiles,),
        in_specs=[
            hspec, hspec,
            pl.BlockSpec((TOK_TILE, d), lambda i: (i, 0)),
            pl.BlockSpec((TOK_TILE, d), lambda i: (i, 0)),
            pl.BlockSpec(mod.shape, const),
            pl.BlockSpec((1, dk), const),
            pl.BlockSpec((d, d), const),
        ],
        out_specs=pl.BlockSpec((TOK_TILE, d), lambda i: (i, 0)),
        compiler_params=_params("arbitrary"),
        name="dn_out",
    )(o_dir[0], o_dir[1], z, xflat, mod, dn_norm.reshape(1, dk), w_out.astype(BF16))


def kernel(x, c, ctx, c_ctx, w_ada, b_ada, norm_mix, norm_ffn, w_pool, b_pool, pool_scale, w_dn_in, w_dn_conv, dn_a_log, dn_dt_bias, dn_norm, w_dn_out, w_rg, b_rg, w_re, b_re, w_e_gate, w_e_up, w_e_down, norm_final):
    bsz, n, d = x.shape
    c_len = ctx.shape[1]
    assert w_ada.shape[0] == 2 and bsz < COND_ROWS and n % POOL_TILE == 0 and c_len == 256
    assert (bsz * c_len) % TOK_TILE == 0 and d % (4 * LANES) == 0
    t_lat = bsz * n
    t_all = t_lat + bsz * c_len
    ctx_row = bsz
    cond = jnp.zeros((COND_ROWS, d), F32).at[:bsz].set(c.astype(F32)).at[ctx_row].set(c_ctx.astype(F32))
    mod = _ada(cond, w_ada, b_ada)

    xflat = _pool_layer(x.astype(F32), ctx.astype(F32), mod[0], norm_mix[0], w_pool[0], b_pool[0], pool_scale[0])
    xflat = _moe_layer(xflat, t_all, mod[0], norm_ffn[0], w_rg[0], b_rg[0], w_re[0], b_re[0],
                       w_e_gate, w_e_up, w_e_down, 0, t_lat=t_lat, n=n, ctx_row=ctx_row, norm_final=None)

    xlat = _deltanet_layer(xflat, bsz, n, c_len, mod[1], norm_mix[1], w_dn_in[0], w_dn_conv[0], dn_a_log[0],
                           dn_dt_bias[0], dn_norm[0], w_dn_out[0])
    out = _moe_layer(xlat, t_lat, mod[1], norm_ffn[1], w_rg[1], b_rg[1], w_re[1], b_re[1],
                     w_e_gate, w_e_up, w_e_down, 1, t_lat=t_lat, n=n, ctx_row=ctx_row, norm_final=norm_final)
    return out.reshape(bsz, n, d).astype(x.dtype)
```

```python
import functools

import numpy as np
import jax
import jax.numpy as jnp
from jax import lax
from jax.experimental import pallas as pl
from jax.experimental.pallas import tpu as pltpu

GRID_W = 64
POOL_WINDOWS = (2, 4, 8, 16)
DN_HEADS = 8
DN_CONV_W = 4
DN_CHUNK = 64
DN_PREP_CHUNKS = 2
N_GROUPS = 4
EXPERTS_PER_GROUP = 8
EPS = 1e-6

LANES = 128
SUBLANES = 8
VMEM_LIMIT_BYTES = 56 * 1024 * 1024

COND_ROWS = 8
TOK_TILE = 512
POOL_TILE = 1024
POOL_HALO = 512
MOE_TILE = 512
ROW_UNROLL = 8
GROUP = 8
BF16 = jnp.bfloat16
F32 = jnp.float32
HIGHEST = lax.Precision.HIGHEST


def _dot(a, b):
    return jnp.dot(a, b, preferred_element_type=F32)


def _dot_hi(a, b):
    return jnp.dot(a, b, precision=HIGHEST, preferred_element_type=F32)


def _split_bf16(a):
    hi = a.astype(BF16)
    lo = (a - hi.astype(F32)).astype(BF16)
    return hi, lo


def _dot3(a, b_hi, b_lo):
    a_hi, a_lo = _split_bf16(a)
    return _dot(a_hi, b_hi) + _dot(a_lo, b_hi) + _dot(a_hi, b_lo)


def _params(*sem):
    return pltpu.CompilerParams(dimension_semantics=sem, vmem_limit_bytes=VMEM_LIMIT_BYTES)


def _norm_mod(x, nw, sc, sh):
    return x * lax.rsqrt(jnp.mean(x * x, axis=-1, keepdims=True) + EPS) * nw * (1.0 + sc) + sh


def _silu(x):
    return x * jax.nn.sigmoid(x)


def _ada_kernel(cond_ref, w_ref, b_ref, o_ref):
    o_ref[...] = _dot_hi(_silu(cond_ref[...]), w_ref[...]) + b_ref[...]


def _ada(cond, w_ada, b_ada):
    depth, d, d6 = w_ada.shape
    tn = d6 // 4
    return pl.pallas_call(
        _ada_kernel,
        out_shape=jax.ShapeDtypeStruct((depth, COND_ROWS, d6), F32),
        grid=(depth, d6 // tn),
        in_specs=[
            pl.BlockSpec((COND_ROWS, d), lambda i, j: (0, 0)),
            pl.BlockSpec((None, d, tn), lambda i, j: (i, 0, j)),
            pl.BlockSpec((None, 1, tn), lambda i, j: (i, 0, j)),
        ],
        out_specs=pl.BlockSpec((None, COND_ROWS, tn), lambda i, j: (i, 0, j)),
        compiler_params=_params("arbitrary", "arbitrary"),
        name="ada",
    )(cond, w_ada, b_ada.reshape(depth, 1, d6))


def _pool_group_out(d, whi_ref, wlo_ref, g):
    dhi, dlo = _split_bf16(d)
    return _dot(dhi, whi_ref[g]) + _dot(dlo, whi_ref[g]) + _dot(dhi, wlo_ref[g])


def _pool_lat_kernel(x_ref, xp_ref, xn_ref, mod_ref, nw_ref, whi_ref, wlo_ref, bp_ref, ps_ref, cm_ref, zero_ref,
                     o_ref, h_ref, *, d, n_rows):
    del zero_ref
    b = pl.program_id(0)
    i = pl.program_id(1)
    nt = pl.num_programs(1)
    gc = d // len(POOL_WINDOWS)
    sh = mod_ref[pl.ds(b, 1), 0:d]
    sc = mod_ref[pl.ds(b, 1), d:2 * d]
    gt = mod_ref[pl.ds(b, 1), 2 * d:3 * d]
    nw = nw_ref[...]
    h_ref[0:POOL_HALO] = jnp.where(i > 0, _norm_mod(xp_ref[...], nw, sc, sh), 0.0)
    h_ref[POOL_HALO:POOL_HALO + POOL_TILE] = _norm_mod(x_ref[...], nw, sc, sh)
    h_ref[POOL_HALO + POOL_TILE:] = jnp.where(i < nt - 1, _norm_mod(xn_ref[...], nw, sc, sh), 0.0)

    tok = lax.broadcasted_iota(jnp.int32, (POOL_TILE, 1), 0)
    r = i * (POOL_TILE // GRID_W) + (tok >> 6)
    c = tok & (GRID_W - 1)
    blk = cm_ref.shape[-1]
    for g, win in enumerate(POOL_WINDOWS):
        half = win // 2
        cols = slice(g * gc, (g + 1) * gc)
        rs = None
        for o in range(-half, half):
            part = h_ref[pl.ds(POOL_HALO + o * GRID_W, POOL_TILE), cols]
            rs = part if rs is None else rs + part
        segs = []
        for s in range(POOL_TILE // blk):
            hi, lo = _split_bf16(rs[s * blk:(s + 1) * blk])
            segs.append(_dot(cm_ref[g], hi) + _dot(cm_ref[g], lo))
        box = jnp.concatenate(segs, axis=0)
        cnt = ((jnp.minimum(r + half, n_rows) - jnp.maximum(r - half, 0))
               * (jnp.minimum(c + half, GRID_W) - jnp.maximum(c - half, 0))).astype(F32)
        dd = box / cnt - h_ref[POOL_HALO:POOL_HALO + POOL_TILE, cols]
        y = _pool_group_out(dd, whi_ref, wlo_ref, g)
        o_ref[:, cols] = x_ref[:, cols] + gt[:, cols] * ((y + bp_ref[:, cols]) * ps_ref[:, cols])


def _pool_ctx_kernel(x_ref, mod_ref, nw_ref, whi_ref, wlo_ref, bp_ref, ps_ref, cm_ref, prev_ref, o_ref, *, d, ctx_row):
    del prev_ref
    gc = d // len(POOL_WINDOWS)
    length = x_ref.shape[0]
    sh = mod_ref[ctx_row:ctx_row + 1, 0:d]
    sc = mod_ref[ctx_row:ctx_row + 1, d:2 * d]
    gt = mod_ref[ctx_row:ctx_row + 1, 2 * d:3 * d]
    x = x_ref[...]
    h = _norm_mod(x, nw_ref[...], sc, sh)
    t = lax.broadcasted_iota(jnp.int32, (length, 1), 0)
    for g, win in enumerate(POOL_WINDOWS):
        half = win // 2
        cols = slice(g * gc, (g + 1) * gc)
        hg = h[:, cols]
        hi, lo = _split_bf16(hg)
        box = _dot(cm_ref[g], hi) + _dot(cm_ref[g], lo)
        cnt = (jnp.minimum(t + half, length) - jnp.maximum(t - half, 0)).astype(F32)
        dd = box / cnt - hg
        y = _pool_group_out(dd, whi_ref, wlo_ref, g)
        o_ref[:, cols] = x[:, cols] + gt[:, cols] * ((y + bp_ref[:, cols]) * ps_ref[:, cols])


def _band_matrices(n, period):
    t = np.arange(n)
    out = []
    for win in POOL_WINDOWS:
        half = win // 2
        diff = t[None, :] - t[:, None]
        same = (t[None, :] // period) == (t[:, None] // period)
        out.append(((diff >= -half) & (diff < half) & same).astype(np.float32))
    return jnp.asarray(np.stack(out), dtype=BF16)


def _pool_layer(x, ctx, mod, nw, w_pool, b_pool, pool_scale):
    bsz, n, d = x.shape
    c_len = ctx.shape[1]
    n_groups = len(POOL_WINDOWS)
    gc = d // n_groups
    t_lat, t_ctx = bsz * n, bsz * c_len
    whi, wlo = _split_bf16(w_pool)
    nw2, bp2, ps2 = nw.reshape(1, d), b_pool.reshape(1, d), pool_scale.reshape(1, d)
    nt = n // POOL_TILE
    nh = n // POOL_HALO
    ratio = POOL_TILE // POOL_HALO
    const2 = lambda b, i: (0, 0)
    const3 = lambda b, i: (0, 0, 0)
    lat = pl.pallas_call(
        functools.partial(_pool_lat_kernel, d=d, n_rows=n // GRID_W),
        out_shape=jax.ShapeDtypeStruct((t_lat + t_ctx, d), F32),
        grid=(bsz, nt),
        in_specs=[
            pl.BlockSpec((None, POOL_TILE, d), lambda b, i: (b, i, 0)),
            pl.BlockSpec((None, POOL_HALO, d), lambda b, i: (b, jnp.maximum(ratio * i - 1, 0), 0)),
            pl.BlockSpec((None, POOL_HALO, d), lambda b, i: (b, jnp.minimum(ratio * i + ratio, nh - 1), 0)),
            pl.BlockSpec(mod.shape, const2),
            pl.BlockSpec((1, d), const2),
            pl.BlockSpec((n_groups, gc, gc), const3),
            pl.BlockSpec((n_groups, gc, gc), const3),
            pl.BlockSpec((1, d), const2),
            pl.BlockSpec((1, d), const2),
            pl.BlockSpec((n_groups, 256, 256), const3),
            pl.BlockSpec(memory_space=pl.ANY),
        ],
        out_specs=pl.BlockSpec((POOL_TILE, d), lambda b, i: (b * nt + i, 0)),
        scratch_shapes=[pltpu.VMEM((POOL_TILE + 2 * POOL_HALO, d), F32)],
        input_output_aliases={10: 0},
        compiler_params=_params("arbitrary", "arbitrary"),
        name="pool_lat",
    )(x, x, x, mod, nw2, whi, wlo, bp2, ps2, _band_matrices(256, GRID_W), jnp.zeros((t_lat + t_ctx, d), F32))
    const1 = lambda b: (0, 0)
    const13 = lambda b: (0, 0, 0)
    return pl.pallas_call(
        functools.partial(_pool_ctx_kernel, d=d, ctx_row=bsz),
        out_shape=jax.ShapeDtypeStruct((t_lat + t_ctx, d), F32),
        grid=(bsz,),
        in_specs=[
            pl.BlockSpec((None, c_len, d), lambda b: (b, 0, 0)),
            pl.BlockSpec(mod.shape, const1),
            pl.BlockSpec((1, d), const1),
            pl.BlockSpec((n_groups, gc, gc), const13),
            pl.BlockSpec((n_groups, gc, gc), const13),
            pl.BlockSpec((1, d), const1),
            pl.BlockSpec((1, d), const1),
            pl.BlockSpec((n_groups, c_len, c_len), const13),
            pl.BlockSpec(memory_space=pl.ANY),
        ],
        out_specs=pl.BlockSpec((c_len, d), lambda b: (t_lat // c_len + b, 0)),
        input_output_aliases={8: 0},
        compiler_params=_params("arbitrary"),
        name="pool_ctx",
    )(ctx, mod, nw2, whi, wlo, bp2, ps2, _band_matrices(c_len, c_len), lat)


def _for_rows(n, fn):
    def body(step, carry):
        base = pl.multiple_of(step * ROW_UNROLL, ROW_UNROLL)
        for u in range(ROW_UNROLL):
            fn(base + u)
        return carry

    lax.fori_loop(0, n // ROW_UNROLL, body, 0)


def _for_count(n, fn):
    def body(j, carry):
        fn(j)
        return carry

    lax.fori_loop(0, n, body, 0)


def _group(ref, g):
    return ref.at[pl.ds(pl.multiple_of(g * GROUP, GROUP), GROUP)]


def _mod_row(tile, lat_tiles, tiles_per_batch, ctx_row):
    return jnp.where(tile < lat_tiles, tile // tiles_per_batch, ctx_row)


def _route_kernel(x_ref, mod_ref, nw_ref, wrh_ref, wrl_ref, br_ref, ls_ref, up_ref,
                  rtok_ref, rt_ref, ng_ref, cb_ref, cnt_ref, carry_ref, *, d, lat_tiles, tiles_per_batch, ctx_row):
    i = pl.program_id(0)
    row = _mod_row(i, lat_tiles, tiles_per_batch, ctx_row)
    sh = mod_ref[pl.ds(row, 1), 3 * d:4 * d]
    sc = mod_ref[pl.ds(row, 1), 4 * d:5 * d]
    h = _norm_mod(x_ref[...], nw_ref[...], sc, sh)
    logits = _dot3(h, wrh_ref[...], wrl_ref[...]) + br_ref[...]
    lane = lax.broadcasted_iota(jnp.int32, logits.shape, 1).astype(F32)
    neg = jnp.float32(-jnp.inf)
    big = jnp.float32(1e9)
    is_g = lane < N_GROUPS
    lg = jnp.where(is_g, logits, neg)
    gmax = jnp.max(lg, axis=1, keepdims=True)
    g_idx = jnp.min(jnp.where(lg == gmax, lane, big), axis=1, keepdims=True)
    pg_top = 1.0 / jnp.sum(jnp.where(is_g, jnp.exp(lg - gmax), 0.0), axis=1, keepdims=True)
    lo = N_GROUPS + EXPERTS_PER_GROUP * g_idx
    le = jnp.where((lane >= lo) & (lane < lo + EXPERTS_PER_GROUP), logits, neg)
    e1 = jnp.max(le, axis=1, keepdims=True)
    i1 = jnp.min(jnp.where(le == e1, lane, big), axis=1, keepdims=True)
    le2 = jnp.where(lane == i1, neg, le)
    e2 = jnp.max(le2, axis=1, keepdims=True)
    i2 = jnp.min(jnp.where(le2 == e2, lane, big), axis=1, keepdims=True)
    r21 = jnp.exp(e2 - e1)
    w1 = pg_top / (1.0 + r21)
    w2 = pg_top * r21 / (1.0 + r21)
    id1 = i1 - N_GROUPS
    id2 = i2 - N_GROUPS

    @pl.when(i == 0)
    def _():
        carry_ref[...] = jnp.zeros_like(carry_ref)

    oh1 = (lane == id1).astype(F32)
    oh2 = (lane == id2).astype(F32)
    cum1 = _dot(ls_ref[...], oh1.astype(BF16))
    cum2 = _dot(ls_ref[...], oh2.astype(BF16))
    tot1 = jnp.sum(oh1, axis=0, keepdims=True)
    tot2 = jnp.sum(oh2, axis=0, keepdims=True)
    ng = jnp.floor((tot1 + tot2 + (GROUP - 1.0)) * (1.0 / GROUP))
    ng8 = jnp.broadcast_to(ng, (SUBLANES, ng.shape[1])).astype(BF16)
    lstart = GROUP * _dot(ng8, up_ref[...])[0:1]
    lpos1 = jnp.sum(oh1 * (cum1 + lstart), axis=1, keepdims=True)
    lpos2 = jnp.sum(oh2 * (cum2 + tot1 + lstart), axis=1, keepdims=True)
    carry = carry_ref[...]
    ng_ref[0] = ng
    cb_ref[0] = carry
    carry = carry + ng
    carry_ref[...] = carry
    cnt_ref[...] = carry

    fields = (id1, id2, lpos1, lpos2, w1, w2)
    slab = jnp.zeros_like(logits)
    for k, f in enumerate(fields):
        slab = jnp.where(lane == k, f, slab)
    rtok_ref[...] = slab
    rt_ref[...] = slab.T[0:SUBLANES]


def _dispatch_kernel(zl_ref, tab_ref, x_ref, mod_ref, nw_ref, o_ref, h_ref, l_ref, z_ref, cnt_ref, sem, zsem,
                     *, d, lat_tiles, tiles_per_batch, ctx_row, n_zero_max):
    i = pl.program_id(0)
    tab_n = tab_ref.shape[-1]
    slot = i % 2

    @pl.when(i == 0)
    def _():
        l_ref[...] = jnp.zeros_like(l_ref)
        z_ref[...] = jnp.zeros_like(z_ref)

        def zcopy(z):
            return pltpu.make_async_copy(z_ref, _group(o_ref, zl_ref[z]), zsem)

        _for_count(zl_ref[n_zero_max], lambda z: zcopy(z).start())
        _for_count(zl_ref[n_zero_max], lambda z: zcopy(z).wait())

    row = _mod_row(i, lat_tiles, tiles_per_batch, ctx_row)
    sh = mod_ref[pl.ds(row, 1), 3 * d:4 * d]
    sc = mod_ref[pl.ds(row, 1), 4 * d:5 * d]
    h_ref[...] = _norm_mod(x_ref[...], nw_ref[...], sc, sh).reshape(h_ref.shape)

    def place(r):
        v = h_ref[r]
        l_ref[slot, tab_ref[0, 0, r]] = v
        l_ref[slot, tab_ref[0, 0, TOK_TILE + r]] = v

    _for_rows(TOK_TILE, place)

    def gcopy(s, j, dst):
        return pltpu.make_async_copy(_group(l_ref.at[s], j), _group(o_ref, dst), sem.at[s])

    n_groups = tab_ref[0, 0, tab_n - 1]
    _for_count(n_groups, lambda j: gcopy(slot, j, tab_ref[0, 0, 2 * TOK_TILE + j]).start())

    @pl.when(i > 0)
    def _():
        _for_count(cnt_ref[0], lambda j: gcopy(1 - slot, j, 0).wait())

    cnt_ref[0] = n_groups

    @pl.when(i == pl.num_programs(0) - 1)
    def _():
        _for_count(n_groups, lambda j: gcopy(slot, j, 0).wait())


def _gmm_kernel(te_ref, nu_ref, x_ref, wg_ref, wu_ref, wd_ref, o_ref, wgb_ref, wub_ref, wdb_ref):
    i = pl.program_id(0)

    @pl.when((i == 0) | (te_ref[i] != te_ref[jnp.maximum(i - 1, 0)]))
    def _():
        wgb_ref[...] = wg_ref[...].astype(BF16)
        wub_ref[...] = wu_ref[...].astype(BF16)
        wdb_ref[...] = wd_ref[...].astype(BF16)

    @pl.when(i < nu_ref[0])
    def _():
        x = x_ref[...].reshape(x_ref.shape[0], wgb_ref.shape[0]).astype(BF16)
        hid = _silu(_dot(x, wgb_ref[...])) * _dot(x, wub_ref[...])
        o_ref[...] = _dot(hid.astype(BF16), wdb_ref[...]).reshape(o_ref.shape)

    @pl.when(i >= nu_ref[0])
    def _():
        o_ref[...] = jnp.zeros_like(o_ref)


def _combine_kernel(tab_ref, nxt_ref, x_ref, rtok_ref, mod_ref, nf_ref, y_ref, o_ref, l_ref, g1_ref, g2_ref, sem,
                    *, d, lat_tiles, tiles_per_batch, ctx_row, final_norm):
    i = pl.program_id(0)
    tab_n = tab_ref.shape[-1]
    slot = i % 2
    row = _mod_row(i, lat_tiles, tiles_per_batch, ctx_row)
    gt = mod_ref[pl.ds(row, 1), 5 * d:6 * d]

    def gcopy(s, j, src):
        return pltpu.make_async_copy(_group(y_ref, src), _group(l_ref.at[s], j), sem.at[s])

    @pl.when(i == 0)
    def _():
        _for_count(tab_ref[0, 0, tab_n - 1], lambda j: gcopy(0, j, tab_ref[0, 0, 2 * TOK_TILE + j]).start())

    @pl.when(i < pl.num_programs(0) - 1)
    def _():
        _for_count(nxt_ref[0, 0, tab_n - 1], lambda j: gcopy(1 - slot, j, nxt_ref[0, 0, 2 * TOK_TILE + j]).start())

    _for_count(tab_ref[0, 0, tab_n - 1], lambda j: gcopy(slot, j, 0).wait())

    def pick(r):
        g1_ref[r] = l_ref[slot, tab_ref[0, 0, r]]
        g2_ref[r] = l_ref[slot, tab_ref[0, 0, TOK_TILE + r]]

    _for_rows(TOK_TILE, pick)
    rt = rtok_ref[...]
    w1 = rt[:, 4:5]
    w2 = rt[:, 5:6]
    out = x_ref[...] + gt * (w1 * g1_ref[...].reshape(x_ref.shape) + w2 * g2_ref[...].reshape(x_ref.shape))
    if final_norm:
        out = out * lax.rsqrt(jnp.mean(out * out, axis=-1, keepdims=True) + EPS) * nf_ref[...]
    o_ref[...] = out


def _moe_layer(xflat, n_tok, mod, nw, w_rg, b_rg, w_re, b_re, wg, wu, wd, layer, *, t_lat, n, ctx_row, norm_final):
    d = xflat.shape[1]
    n_exp = w_re.shape[1]
    tiles = n_tok // TOK_TILE
    lat_tiles = t_lat // TOK_TILE
    tiles_per_batch = n // TOK_TILE
    common = dict(d=d, lat_tiles=lat_tiles, tiles_per_batch=tiles_per_batch, ctx_row=ctx_row)
    nw2 = nw.reshape(1, d)
    wr = jnp.zeros((d, LANES), F32).at[:, :N_GROUPS].set(w_rg).at[:, N_GROUPS:N_GROUPS + n_exp].set(w_re)
    br = jnp.zeros((1, LANES), F32).at[0, :N_GROUPS].set(b_rg).at[0, N_GROUPS:N_GROUPS + n_exp].set(b_re)
    wrh, wrl = _split_bf16(wr)
    lstrict = jnp.asarray(np.tril(np.ones((TOK_TILE, TOK_TILE), np.float32), -1), dtype=BF16)
    upper = jnp.asarray(np.triu(np.ones((LANES, LANES), np.float32), 1), dtype=BF16)
    const = lambda i: (0, 0)
    per_tile = pl.BlockSpec((1, 1, LANES), lambda i: (i, 0, 0))
    rtok, rt, ngf, cbf, cnt = pl.pallas_call(
        functools.partial(_route_kernel, **common),
        out_shape=(jax.ShapeDtypeStruct((n_tok, LANES), F32),
                   jax.ShapeDtypeStruct((SUBLANES, n_tok), F32),
                   jax.ShapeDtypeStruct((tiles, 1, LANES), F32),
                   jax.ShapeDtypeStruct((tiles, 1, LANES), F32),
                   jax.ShapeDtypeStruct((1, LANES), F32)),
        grid=(tiles,),
        in_specs=[
            pl.BlockSpec((TOK_TILE, d), lambda i: (i, 0)),
            pl.BlockSpec(mod.shape, const),
            pl.BlockSpec((1, d), const),
            pl.BlockSpec((d, LANES), const),
            pl.BlockSpec((d, LANES), const),
            pl.BlockSpec((1, LANES), const),
            pl.BlockSpec((TOK_TILE, TOK_TILE), const),
            pl.BlockSpec((LANES, LANES), const),
        ],
        out_specs=(pl.BlockSpec((TOK_TILE, LANES), lambda i: (i, 0)),
                   pl.BlockSpec((SUBLANES, TOK_TILE), lambda i: (0, i)),
                   per_tile, per_tile,
                   pl.BlockSpec((1, LANES), const)),
        scratch_shapes=[pltpu.VMEM((1, LANES), F32)],
        compiler_params=_params("arbitrary"),
        name="moe_route",
    )(xflat, mod, nw2, wrh, wrl, br, lstrict, upper)

    i32 = jnp.int32
    experts = jnp.arange(n_exp, dtype=i32)
    ng_t = ngf[:, 0, :n_exp].astype(i32)
    cb_t = cbf[:, 0, :n_exp].astype(i32)
    tot_g = cnt[0, :n_exp].astype(i32)
    gpt = MOE_TILE // GROUP
    local_groups = (2 * TOK_TILE) // GROUP + n_exp
    n_sorted_tiles = pl.cdiv((2 * n_tok) // GROUP + n_exp * tiles, gpt) + n_exp
    n_sorted_groups = n_sorted_tiles * gpt
    padded_g = ((tot_g + gpt - 1) // gpt) * gpt
    ends_g = jnp.cumsum(padded_g)
    off_g = ends_g - padded_g
    n_used = (ends_g[-1] // gpt).astype(i32).reshape(1)
    tile_ids = jnp.arange(n_sorted_tiles, dtype=i32)
    tile_expert = jnp.sum(tile_ids[:, None] * gpt >= ends_g[None, :], axis=1).astype(i32)
    last_expert = jnp.sum(jnp.maximum(n_used - 1, 0) * gpt >= ends_g).astype(i32)
    tile_expert = jnp.minimum(tile_expert, last_expert)
    lend = jnp.cumsum(ng_t, axis=1)
    jg = jnp.arange(local_groups, dtype=i32)
    seg = jnp.sum(jg[None, :, None] >= lend[:, None, :], axis=-1).astype(i32)
    base = off_g[None, :] + cb_t - (lend - ng_t)
    dst = jnp.sum(jnp.where(seg[:, :, None] == experts, base[:, None, :], 0), axis=-1) + jg[None, :]
    dst = jnp.where(seg < n_exp, dst, 0)
    lpos = rt[2:4].astype(i32).reshape(2, tiles, TOK_TILE).transpose(1, 0, 2).reshape(tiles, 2 * TOK_TILE)
    tab_n = 2 * TOK_TILE + pl.cdiv(local_groups + 1, LANES) * LANES
    fill = jnp.zeros((tiles, tab_n - 2 * TOK_TILE - local_groups - 1), i32)
    tab = jnp.concatenate([lpos, dst.astype(i32), fill, lend[:, -1:]], axis=1).reshape(tiles, 1, tab_n)
    n_zero_max = n_sorted_groups - (2 * n_tok) // GROUP
    zc = jnp.concatenate([padded_g - tot_g, n_sorted_groups - ends_g[-1:]])
    zs = jnp.concatenate([off_g + tot_g, ends_g[-1:]])
    zend = jnp.cumsum(zc)
    zi = jnp.arange(n_zero_max, dtype=i32)
    zseg = jnp.sum(zi[:, None] >= zend[None, :], axis=-1).astype(i32)
    zl = jnp.sum(jnp.where(zseg[:, None] == jnp.arange(n_exp + 1, dtype=i32), (zs - (zend - zc))[None, :], 0), axis=-1) + zi
    zlist = jnp.concatenate([jnp.where(zi < zend[-1], zl, 0), zend[-1:]]).astype(i32)

    n_sorted = n_sorted_groups * GROUP
    row = (d // LANES, LANES)
    local_rows = local_groups * GROUP
    xs_sorted = pl.pallas_call(
        functools.partial(_dispatch_kernel, n_zero_max=n_zero_max, **common),
        out_shape=jax.ShapeDtypeStruct((n_sorted,) + row, F32),
        grid_spec=pltpu.PrefetchScalarGridSpec(
            num_scalar_prefetch=1,
            grid=(tiles,),
            in_specs=[
                pl.BlockSpec((1, 1, tab_n), lambda i, zl: (i, 0, 0), memory_space=pltpu.SMEM),
                pl.BlockSpec((TOK_TILE, d), lambda i, zl: (i, 0)),
                pl.BlockSpec(mod.shape, lambda i, zl: (0, 0)),
                pl.BlockSpec((1, d), lambda i, zl: (0, 0)),
            ],
            out_specs=pl.BlockSpec(memory_space=pl.ANY),
            scratch_shapes=[pltpu.VMEM((TOK_TILE,) + row, F32), pltpu.VMEM((2, local_rows) + row, F32),
                            pltpu.VMEM((GROUP,) + row, F32), pltpu.SMEM((1,), jnp.int32),
                            pltpu.SemaphoreType.DMA((2,)), pltpu.SemaphoreType.DMA(())],
        ),
        compiler_params=_params("arbitrary"),
        name="moe_dispatch",
    )(zlist, tab, xflat, mod, nw2)

    de = wg.shape[-1]
    y_sorted = pl.pallas_call(
        _gmm_kernel,
        out_shape=jax.ShapeDtypeStruct((n_sorted,) + row, F32),
        grid_spec=pltpu.PrefetchScalarGridSpec(
            num_scalar_prefetch=2,
            grid=(n_sorted_tiles,),
            in_specs=[
                pl.BlockSpec((MOE_TILE,) + row, lambda i, te, nu: (jnp.minimum(i, jnp.maximum(nu[0] - 1, 0)), 0, 0)),
                pl.BlockSpec((None, None, d, de), lambda i, te, nu: (layer, te[i], 0, 0)),
                pl.BlockSpec((None, None, d, de), lambda i, te, nu: (layer, te[i], 0, 0)),
                pl.BlockSpec((None, None, de, d), lambda i, te, nu: (layer, te[i], 0, 0)),
            ],
            out_specs=pl.BlockSpec((MOE_TILE,) + row, lambda i, te, nu: (i, 0, 0)),
            scratch_shapes=[pltpu.VMEM((d, de), BF16), pltpu.VMEM((d, de), BF16), pltpu.VMEM((de, d), BF16)],
        ),
        compiler_params=_params("arbitrary"),
        name="moe_gmm",
    )(tile_expert, n_used, xs_sorted, wg, wu, wd)

    final = norm_final is not None
    nf = (norm_final if final else jnp.ones((d,), F32)).reshape(1, d)
    return pl.pallas_call(
        functools.partial(_combine_kernel, final_norm=final, **common),
        out_shape=jax.ShapeDtypeStruct((n_tok, d), F32),
        grid=(tiles,),
        in_specs=[
            pl.BlockSpec((1, 1, tab_n), lambda i: (i, 0, 0), memory_space=pltpu.SMEM),
            pl.BlockSpec((1, 1, tab_n), lambda i: (jnp.minimum(i + 1, tiles - 1), 0, 0), memory_space=pltpu.SMEM),
            pl.BlockSpec((TOK_TILE, d), lambda i: (i, 0)),
            pl.BlockSpec((TOK_TILE, LANES), lambda i: (i, 0)),
            pl.BlockSpec(mod.shape, const),
            pl.BlockSpec((1, d), const),
            pl.BlockSpec(memory_space=pl.ANY),
        ],
        out_specs=pl.BlockSpec((TOK_TILE, d), lambda i: (i, 0)),
        scratch_shapes=[pltpu.VMEM((2, local_rows) + row, F32), pltpu.VMEM((TOK_TILE,) + row, F32),
                        pltpu.VMEM((TOK_TILE,) + row, F32), pltpu.SemaphoreType.DMA((2,))],
        compiler_params=_params("arbitrary"),
        name="moe_combine",
    )(tab, tab, xflat, rtok, mod, nf, y_sorted)


def _dn_proj_kernel(x_ref, xp_ref, xn_ref, mod_ref, nw_ref, wqkv_ref, wz_ref, wabh_ref, wabl_ref, wconv_ref, gpar_ref,
                    tril_ref, q_ref, k_ref, v_ref, z_ref, gct_ref, gcr_ref, hf_ref,
                    *, d, tile, tiles_per_seq, lat, ctx_row):
    i = pl.program_id(0)
    row = (i // tiles_per_seq) if lat else ctx_row
    sh = mod_ref[pl.ds(row, 1), 0:d]
    sc = mod_ref[pl.ds(row, 1), d:2 * d]
    nw = nw_ref[...]
    first = (i % tiles_per_seq) == 0
    last = (i % tiles_per_seq) == tiles_per_seq - 1
    halo = SUBLANES
    hf_ref[0:halo] = jnp.where(first, 0.0, _norm_mod(xp_ref[...], nw, sc, sh))
    hm = _norm_mod(x_ref[...], nw, sc, sh)
    hf_ref[halo:halo + tile] = hm
    hf_ref[halo + tile:] = jnp.where(last, 0.0, _norm_mod(xn_ref[...], nw, sc, sh))
    hb = hf_ref[...].astype(BF16)

    z_ref[...] = _dot(hm.astype(BF16), wz_ref[...])

    ab = _dot3(hm, wabh_ref[...], wabl_ref[...])
    a_log, dt_bias, is_g, is_f, is_b = (gpar_ref[j:j + 1, :] for j in range(5))
    xg = ab + dt_bias
    softplus = jnp.maximum(xg, 0.0) + jnp.log(1.0 + jnp.exp(-jnp.abs(xg)))
    slab = jnp.where(is_g > 0.5, -jnp.exp(a_log) * softplus, jax.nn.sigmoid(ab))
    s_hi, s_lo = _split_bf16(slab)
    pre, tot = [], []
    for cc in range(tile // DN_CHUNK):
        rows = slice(cc * DN_CHUNK, (cc + 1) * DN_CHUNK)
        p = _dot(tril_ref[...], s_hi[rows]) + _dot(tril_ref[...], s_lo[rows])
        pre.append(p)
        tot.append(jnp.broadcast_to(p[DN_CHUNK - 1:DN_CHUNK], p.shape))
    gpre = jnp.concatenate(pre, axis=0)
    gsuf = jnp.concatenate(tot, axis=0) - gpre + slab
    gcs = jnp.where(is_f > 0.5, gpre, jnp.where(is_b > 0.5, gsuf, slab))
    gct_ref[...] = gcs
    gcs_t = gcs.T
    n_ab = 4 * DN_HEADS
    for cc in range(tile // DN_CHUNK):
        gcr_ref[cc] = gcs_t[0:n_ab, cc * DN_CHUNK:(cc + 1) * DN_CHUNK]

    dk = d // DN_HEADS
    cw = 4 * dk
    outs = (q_ref, k_ref, v_ref)
    for cc in range(3 * d // cw):
        cols = slice(cc * cw, (cc + 1) * cw)
        pc = _dot(hb, wqkv_ref[:, cols])
        wc = wconv_ref[:, cols]
        n_rows = tile + 2 * halo
        conv = (pltpu.roll(pc, 2, 0)[halo:halo + tile] * wc[0:1]
                + pltpu.roll(pc, 1, 0)[halo:halo + tile] * wc[1:2]
                + pc[halo:halo + tile] * wc[2:3]
                + pltpu.roll(pc, n_rows - 1, 0)[halo:halo + tile] * wc[3:4])
        act = _silu(conv)
        which = (cc * cw) // d
        for hh in range(cw // dk):
            head = ((cc * cw) % d) // dk + hh
            t = act[:, hh * dk:(hh + 1) * dk]
            if which < 2:
                t = t * lax.rsqrt(jnp.sum(t * t, axis=-1, keepdims=True) + EPS)
            if which == 0:
                t = t * (dk ** -0.5)
            outs[which][head] = t


def _dn_proj(xflat, tok_off, n_seq, seq_len, tile, lat, mod, nw, wqkv, wz, wab, wconv, gpar, ctx_row):
    d = xflat.shape[1]
    dk = d // DN_HEADS
    t_n = n_seq * seq_len
    tiles = t_n // tile
    tiles_per_seq = seq_len // tile
    boff = tok_off // tile
    hoff = tok_off // SUBLANES
    hper = tile // SUBLANES
    hmax = xflat.shape[0] // SUBLANES - 1
    tril = jnp.asarray(np.tril(np.ones((DN_CHUNK, DN_CHUNK), np.float32)), dtype=BF16)
    wabh, wabl = _split_bf16(wab)
    const = lambda i: (0, 0)
    hshape = jax.ShapeDtypeStruct((DN_HEADS, t_n, dk), F32)
    hspec = pl.BlockSpec((DN_HEADS, tile, dk), lambda i: (0, i, 0))
    n_ab = 4 * DN_HEADS
    return pl.pallas_call(
        functools.partial(_dn_proj_kernel, d=d, tile=tile, tiles_per_seq=tiles_per_seq, lat=lat, ctx_row=ctx_row),
        out_shape=(hshape, hshape, hshape,
                   jax.ShapeDtypeStruct((t_n, d), F32),
                   jax.ShapeDtypeStruct((t_n, LANES), F32),
                   jax.ShapeDtypeStruct((t_n // DN_CHUNK, n_ab, DN_CHUNK), F32)),
        grid=(tiles,),
        in_specs=[
            pl.BlockSpec((tile, d), lambda i: (boff + i, 0)),
            pl.BlockSpec((SUBLANES, d), lambda i: (jnp.maximum(hoff + i * hper - 1, 0), 0)),
            pl.BlockSpec((SUBLANES, d), lambda i: (jnp.minimum(hoff + (i + 1) * hper, hmax), 0)),
            pl.BlockSpec(mod.shape, const),
            pl.BlockSpec((1, d), const),
            pl.BlockSpec(wqkv.shape, const),
            pl.BlockSpec(wz.shape, const),
            pl.BlockSpec(wab.shape, const),
            pl.BlockSpec(wab.shape, const),
            pl.BlockSpec(wconv.shape, const),
            pl.BlockSpec(gpar.shape, const),
            pl.BlockSpec((DN_CHUNK, DN_CHUNK), const),
        ],
        out_specs=(hspec, hspec, hspec,
                   pl.BlockSpec((tile, d), lambda i: (i, 0)),
                   pl.BlockSpec((tile, LANES), lambda i: (i, 0)),
                   pl.BlockSpec((tile // DN_CHUNK, n_ab, DN_CHUNK), lambda i: (i, 0, 0))),
        scratch_shapes=[pltpu.VMEM((tile + 2 * SUBLANES, d), F32)],
        compiler_params=_params("arbitrary"),
        name="dn_proj_lat" if lat else "dn_proj_ctx",
    )(xflat, xflat, xflat, mod, nw.reshape(1, d), wqkv, wz, wabh, wabl, wconv, gpar, tril)


def _dn_chunk_kernel(q_ref, k_ref, v_ref, gct_ref, gcr_ref, s0_ref, *rest, blk, reverse, need_output):
    n_out = 2 if need_output else 1
    o_ref = rest[0] if need_output else None
    sf_ref, s_ref = rest[n_out - 1], rest[n_out]
    bufs = rest[n_out + 1:]
    sets = (bufs[:len(bufs) // 2], bufs[len(bufs) // 2:])
    j = pl.program_id(1)
    nchunk = blk // DN_CHUNK
    c = DN_CHUNK
    dv = v_ref.shape[-1]
    heads = range(DN_HEADS)

    @pl.when(j == 0)
    def _():
        s_ref[...] = s0_ref[...]

    ii = lax.broadcasted_iota(jnp.int32, (c, c), 0)
    jj = lax.broadcasted_iota(jnp.int32, (c, c), 1)
    strict = (ii < jj) if reverse else (ii > jj)
    incl = (ii <= jj) if reverse else (ii >= jj)
    field = 2 if reverse else 0
    nt_dims = (((1,), (1,)), ((), ()))

    def g_last(gcc):
        return gcc[0:1, :] if reverse else gcc[c - 1:c, :]

    def prep(pair, buf):
        u_s, wq_s, kdt_s = buf[:3]
        inst = []
        for cc in range(DN_PREP_CHUNKS):
            ci = pair * DN_PREP_CHUNKS + cc
            rows = slice(ci * c, (ci + 1) * c)
            gcols = gct_ref[rows, :]
            grows = gcr_ref[ci]
            for h in heads:
                ln = 4 * h + field
                inst.append((cc, rows, h, gcols[:, ln:ln + 1], gcols[:, ln + 1:ln + 2], grows[ln:ln + 1, :]))
        kc = [k_ref[h, rows, :] for _, rows, h, _, _, _ in inst]
        vc = [v_ref[h, rows, :] for _, rows, h, _, _, _ in inst]
        decay = [jnp.exp(jnp.where(incl, gcc - gcr, 0.0)) for _, _, _, gcc, _, gcr in inst]
        egc = [jnp.exp(t[3]) for t in inst]
        kb = [kc_i * t[4] for kc_i, t in zip(kc, inst)]
        if need_output:
            qc = [q_ref[h, rows, :] for _, rows, h, _, _, _ in inst]
            lhs = [jnp.concatenate([kb_i, qc_i], axis=0).astype(BF16) for kb_i, qc_i in zip(kb, qc)]
        else:
            lhs = [kb_i.astype(BF16) for kb_i in kb]
        kq = [lax.dot_general(l_i, kc_i.astype(BF16), nt_dims, preferred_element_type=F32)
              for l_i, kc_i in zip(lhs, kc)]
        e = [jnp.where(strict, -(kq_i[0:c] * d_i), 0.0) for kq_i, d_i in zip(kq, decay)]
        pb = [e_i.astype(BF16) for e_i in e]
        p = [_dot(p_i, p_i) for p_i in pb]
        for level in range(1, 6):
            pb = [p_i.astype(BF16) for p_i in p]
            if level < 5:
                out = [_dot(jnp.concatenate([e_i.astype(BF16), pb_i], axis=0), pb_i) for pb_i, e_i in zip(pb, e)]
                e = [e_i + p_i + o_i[0:c] for e_i, p_i, o_i in zip(e, p, out)]
                p = [o_i[c:2 * c] for o_i in out]
            else:
                e = [e_i + p_i + _dot(e_i.astype(BF16), pb_i) for e_i, p_i, pb_i in zip(e, p, pb)]
        rhs = [jnp.concatenate([vc_i * t[4], kb_i * e_i], axis=1) for vc_i, kb_i, e_i, t in zip(vc, kb, egc, inst)]
        x = [r_i + _dot(e_i.astype(BF16), r_i.astype(BF16)) for e_i, r_i in zip(e, rhs)]
        for n_i, (cc, _, h, gcc, _, _) in enumerate(inst):
            loc = slice(cc * c, (cc + 1) * c)
            u_s[h, loc, :] = x[n_i][:, 0:dv]
            kdt_s[h, cc] = (kc[n_i] * jnp.exp(g_last(gcc) - gcc)).T.astype(BF16)
            if need_output:
                wq_s[h, cc] = jnp.concatenate([x[n_i][:, dv:], qc[n_i] * egc[n_i]], axis=0).astype(BF16)
                buf[3][h, loc, :] = jnp.where(incl, kq[n_i][c:2 * c] * decay[n_i], 0.0).astype(BF16)
            else:
                wq_s[h, cc] = x[n_i][:, dv:].astype(BF16)

    def recur(pair, cc, buf):
        u_s, wq_s, kdt_s = buf[:3]
        ci = pair * DN_PREP_CHUNKS + cc
        rows = slice(ci * c, (ci + 1) * c)
        loc = slice(cc * c, (cc + 1) * c)
        gcols = gct_ref[rows, :]
        s = [s_ref[h] for h in heads]
        sb = [s_h.astype(BF16) for s_h in s]
        ws = [_dot(wq_s[h, cc], sb[h]) for h in heads]
        vb = [(u_s[h, loc, :] - ws[h][0:c]).astype(BF16) for h in heads]
        if need_output:
            for h in heads:
                o_ref[h, rows, :] = ws[h][c:2 * c] + _dot(buf[3][h, loc, :], vb[h])
        for h in heads:
            eg = jnp.exp(g_last(gcols[:, 4 * h + field:4 * h + field + 1]))
            s_ref[h] = s[h] * eg + _dot(kdt_s[h, cc], vb[h])

    pairs = list(range(nchunk // DN_PREP_CHUNKS))
    within = list(range(DN_PREP_CHUNKS))
    if reverse:
        pairs.reverse()
        within.reverse()
    prep(pairs[0], sets[0])
    for n_p, pair in enumerate(pairs):
        for cc in within:
            recur(pair, cc, sets[n_p % 2])
        if n_p + 1 < len(pairs):
            prep(pairs[n_p + 1], sets[(n_p + 1) % 2])

    @pl.when(j == pl.num_programs(1) - 1)
    def _():
        sf_ref[...] = s_ref[...]


def _dn_scan(q, k, v, gct, gcr, s0, n_seq, seq_len, blk, reverse, need_output):
    dk = q.shape[-1]
    nblk = seq_len // blk
    order = (lambda j: nblk - 1 - j) if reverse else (lambda j: j)
    hspec = pl.BlockSpec((DN_HEADS, blk, dk), lambda b, j: (0, b * nblk + order(j), 0))
    sspec = pl.BlockSpec((None, DN_HEADS, dk, dk), lambda b, j: (b, 0, 0, 0))
    n_ab = gcr.shape[1]
    s_shape = jax.ShapeDtypeStruct((n_seq, DN_HEADS, dk, dk), F32)
    out_shape = [s_shape]
    out_specs = [sspec]
    wq_rows = 2 * DN_CHUNK if need_output else DN_CHUNK
    pair_rows = DN_PREP_CHUNKS * DN_CHUNK
    buf_set = [pltpu.VMEM((DN_HEADS, pair_rows, dk), F32),
               pltpu.VMEM((DN_HEADS, DN_PREP_CHUNKS, wq_rows, dk), BF16),
               pltpu.VMEM((DN_HEADS, DN_PREP_CHUNKS, dk, DN_CHUNK), BF16)]
    if need_output:
        out_shape.insert(0, jax.ShapeDtypeStruct(q.shape, F32))
        out_specs.insert(0, hspec)
        buf_set.append(pltpu.VMEM((DN_HEADS, pair_rows, DN_CHUNK), BF16))
    scratch = [pltpu.VMEM((DN_HEADS, dk, dk), F32)] + buf_set + buf_set
    res = pl.pallas_call(
        functools.partial(_dn_chunk_kernel, blk=blk, reverse=reverse, need_output=need_output),
        out_shape=tuple(out_shape),
        grid=(n_seq, nblk),
        in_specs=[
            hspec, hspec, hspec,
            pl.BlockSpec((blk, LANES), lambda b, j: (b * nblk + order(j), 0)),
            pl.BlockSpec((blk // DN_CHUNK, n_ab, DN_CHUNK), lambda b, j: (b * nblk + order(j), 0, 0)),
            sspec,
        ],
        out_specs=tuple(out_specs),
        scratch_shapes=scratch,
        compiler_params=_params("arbitrary", "arbitrary"),
        name="dn_scan_" + ("bwd" if reverse else "fwd") + ("_out" if need_output else "_state"),
    )(q, k, v, gct, gcr, s0)
    return (res[0], res[1]) if need_output else (None, res[0])


def _dn_out_kernel(of_ref, ob_ref, z_ref, x_ref, mod_ref, nrm_ref, wout_ref, o_ref, *, d, tiles_per_batch):
    i = pl.program_id(0)
    row = i // tiles_per_batch
    gt = mod_ref[pl.ds(row, 1), 2 * d:3 * d]
    dk = d // DN_HEADS
    parts = []
    for h in range(DN_HEADS):
        o = of_ref[h] + ob_ref[h]
        o = o * lax.rsqrt(jnp.mean(o * o, axis=-1, keepdims=True) + EPS) * nrm_ref[...]
        parts.append((o * _silu(z_ref[:, h * dk:(h + 1) * dk])).astype(BF16))
    y = _dot(jnp.concatenate(parts, axis=1), wout_ref[...])
    o_ref[...] = x_ref[...] + gt * y


def _deltanet_layer(xflat, bsz, n, c_len, mod, nw, w_in, w_conv, a_log, dt_bias, dn_norm, w_out):
    d = xflat.shape[1]
    dk = d // DN_HEADS
    t_lat = bsz * n
    ctx_row = bsz
    wqkv = w_in[:, :3 * d].astype(BF16)
    wz = w_in[:, 3 * d:4 * d].astype(BF16)
    n_ab = 4 * DN_HEADS
    wab = w_in[:, 4 * d:].reshape(d, 4, DN_HEADS).transpose(0, 2, 1).reshape(d, n_ab)
    wab = jnp.zeros((d, LANES), F32).at[:, :n_ab].set(wab)
    lane = np.arange(LANES)
    fld, head, used = lane % 4, np.minimum(lane // 4, DN_HEADS - 1), lane < n_ab
    is_g = used & (fld % 2 == 0)
    dirn = fld // 2
    gpar = jnp.zeros((SUBLANES, LANES), F32)
    gpar = gpar.at[0].set(jnp.where(is_g, a_log[dirn, head], 0.0))
    gpar = gpar.at[1].set(jnp.where(is_g, dt_bias[dirn, head], 0.0))
    gpar = gpar.at[2].set(jnp.asarray(is_g, F32))
    gpar = gpar.at[3].set(jnp.asarray(used & (fld == 0), F32))
    gpar = gpar.at[4].set(jnp.asarray(used & (fld == 2), F32))

    proj = functools.partial(_dn_proj, xflat, mod=mod, nw=nw, wqkv=wqkv, wz=wz, wab=wab, wconv=w_conv, gpar=gpar,
                             ctx_row=ctx_row)
    q, k, v, z, gct, gcr = proj(tok_off=0, n_seq=bsz, seq_len=n, tile=TOK_TILE, lat=True)
    qc, kc, vc, _, gctc, gcrc = proj(tok_off=t_lat, n_seq=bsz, seq_len=c_len, tile=c_len, lat=False)

    s0 = jnp.zeros((bsz, DN_HEADS, dk, dk), F32)
    o_dir = []
    for reverse in (False, True):
        _, s_ctx = _dn_scan(qc, kc, vc, gctc, gcrc, s0, bsz, c_len, c_len, reverse, False)
        o, _ = _dn_scan(q, k, v, gct, gcr, s_ctx, bsz, n, TOK_TILE, reverse, True)
        o_dir.append(o)

    tiles = t_lat // TOK_TILE
    const = lambda i: (0, 0)
    hspec = pl.BlockSpec((DN_HEADS, TOK_TILE, dk), lambda i: (0, i, 0))
    return pl.pallas_call(
        functools.partial(_dn_out_kernel, d=d, tiles_per_batch=n // TOK_TILE),
        out_shape=jax.ShapeDtypeStruct((t_lat, d), F32),
        grid=(tiles,),
        in_specs=[
            hspec, hspec,
            pl.BlockSpec((TOK_TILE, d), lambda i: (i, 0)),
            pl.BlockSpec((TOK_TILE, d), lambda i: (i, 0)),
            pl.BlockSpec(mod.shape, const),
            pl.BlockSpec((1, dk), const),
            pl.BlockSpec((d, d), const),
        ],
        out_specs=pl.BlockSpec((TOK_TILE, d), lambda i: (i, 0)),
        compiler_params=_params("arbitrary"),
        name="dn_out",
    )(o_dir[0], o_dir[1], z, xflat, mod, dn_norm.reshape(1, dk), w_out.astype(BF16))


def kernel(x, c, ctx, c_ctx, w_ada, b_ada, norm_mix, norm_ffn, w_pool, b_pool, pool_scale, w_dn_in, w_dn_conv, dn_a_log, dn_dt_bias, dn_norm, w_dn_out, w_rg, b_rg, w_re, b_re, w_e_gate, w_e_up, w_e_down, norm_final):
    bsz, n, d = x.shape
    c_len = ctx.shape[1]
    assert w_ada.shape[0] == 2 and bsz < COND_ROWS and n % POOL_TILE == 0 and c_len == 256
    assert (bsz * c_len) % TOK_TILE == 0 and d % (4 * LANES) == 0
    t_lat = bsz * n
    t_all = t_lat + bsz * c_len
    ctx_row = bsz
    cond = jnp.zeros((COND_ROWS, d), F32).at[:bsz].set(c.astype(F32)).at[ctx_row].set(c_ctx.astype(F32))
    mod = _ada(cond, w_ada, b_ada)

    xflat = _pool_layer(x.astype(F32), ctx.astype(F32), mod[0], norm_mix[0], w_pool[0], b_pool[0], pool_scale[0])
    xflat = _moe_layer(xflat, t_all, mod[0], norm_ffn[0], w_rg[0], b_rg[0], w_re[0], b_re[0],
                       w_e_gate, w_e_up, w_e_down, 0, t_lat=t_lat, n=n, ctx_row=ctx_row, norm_final=None)

    xlat = _deltanet_layer(xflat, bsz, n, c_len, mod[1], norm_mix[1], w_dn_in[0], w_dn_conv[0], dn_a_log[0],
                           dn_dt_bias[0], dn_norm[0], w_dn_out[0])
    out = _moe_layer(xlat, t_lat, mod[1], norm_ffn[1], w_rg[1], b_rg[1], w_re[1], b_re[1],
                     w_e_gate, w_e_up, w_e_down, 1, t_lat=t_lat, n=n, ctx_row=ctx_row, norm_final=norm_final)
    return out.reshape(bsz, n, d).astype(x.dtype)
```

```python
import functools

import numpy as np
import jax
import jax.numpy as jnp
from jax import lax
from jax.experimental import pallas as pl
from jax.experimental.pallas import tpu as pltpu

GRID_W = 64
POOL_WINDOWS = (2, 4, 8, 16)
DN_HEADS = 8
DN_CONV_W = 4
DN_CHUNK = 64
DN_PREP_CHUNKS = 2
N_GROUPS = 4
EXPERTS_PER_GROUP = 8
EPS = 1e-6

LANES = 128
SUBLANES = 8
VMEM_LIMIT_BYTES = 56 * 1024 * 1024

COND_ROWS = 8
TOK_TILE = 512
POOL_TILE = 1024
POOL_HALO = 512
MOE_TILE = 512
ROW_UNROLL = 8
GROUP = 8
BF16 = jnp.bfloat16
F32 = jnp.float32
HIGHEST = lax.Precision.HIGHEST


def _dot(a, b):
    return jnp.dot(a, b, preferred_element_type=F32)


def _dot_hi(a, b):
    return jnp.dot(a, b, precision=HIGHEST, preferred_element_type=F32)


def _split_bf16(a):
    hi = a.astype(BF16)
    lo = (a - hi.astype(F32)).astype(BF16)
    return hi, lo


def _dot3(a, b_hi, b_lo):
    a_hi, a_lo = _split_bf16(a)
    return _dot(a_hi, b_hi) + _dot(a_lo, b_hi) + _dot(a_hi, b_lo)


def _params(*sem):
    return pltpu.CompilerParams(dimension_semantics=sem, vmem_limit_bytes=VMEM_LIMIT_BYTES)


def _norm_mod(x, nw, sc, sh):
    return x * lax.rsqrt(jnp.mean(x * x, axis=-1, keepdims=True) + EPS) * nw * (1.0 + sc) + sh


def _silu(x):
    return x * jax.nn.sigmoid(x)


def _ada_kernel(cond_ref, w_ref, b_ref, o_ref):
    o_ref[...] = _dot_hi(_silu(cond_ref[...]), w_ref[...]) + b_ref[...]


def _ada(cond, w_ada, b_ada):
    depth, d, d6 = w_ada.shape
    tn = d6 // 4
    return pl.pallas_call(
        _ada_kernel,
        out_shape=jax.ShapeDtypeStruct((depth, COND_ROWS, d6), F32),
        grid=(depth, d6 // tn),
        in_specs=[
            pl.BlockSpec((COND_ROWS, d), lambda i, j: (0, 0)),
            pl.BlockSpec((None, d, tn), lambda i, j: (i, 0, j)),
            pl.BlockSpec((None, 1, tn), lambda i, j: (i, 0, j)),
        ],
        out_specs=pl.BlockSpec((None, COND_ROWS, tn), lambda i, j: (i, 0, j)),
        compiler_params=_params("arbitrary", "arbitrary"),
        name="ada",
    )(cond, w_ada, b_ada.reshape(depth, 1, d6))


def _pool_group_out(d, whi_ref, wlo_ref, g):
    dhi, dlo = _split_bf16(d)
    return _dot(dhi, whi_ref[g]) + _dot(dlo, whi_ref[g]) + _dot(dhi, wlo_ref[g])


def _pool_lat_kernel(x_ref, xp_ref, xn_ref, mod_ref, nw_ref, whi_ref, wlo_ref, bp_ref, ps_ref, cm_ref, zero_ref,
                     o_ref, h_ref, *, d, n_rows):
    del zero_ref
    b = pl.program_id(0)
    i = pl.program_id(1)
    nt = pl.num_programs(1)
    gc = d // len(POOL_WINDOWS)
    sh = mod_ref[pl.ds(b, 1), 0:d]
    sc = mod_ref[pl.ds(b, 1), d:2 * d]
    gt = mod_ref[pl.ds(b, 1), 2 * d:3 * d]
    nw = nw_ref[...]
    h_ref[0:POOL_HALO] = jnp.where(i > 0, _norm_mod(xp_ref[...], nw, sc, sh), 0.0)
    h_ref[POOL_HALO:POOL_HALO + POOL_TILE] = _norm_mod(x_ref[...], nw, sc, sh)
    h_ref[POOL_HALO + POOL_TILE:] = jnp.where(i < nt - 1, _norm_mod(xn_ref[...], nw, sc, sh), 0.0)

    tok = lax.broadcasted_iota(jnp.int32, (POOL_TILE, 1), 0)
    r = i * (POOL_TILE // GRID_W) + (tok >> 6)
    c = tok & (GRID_W - 1)
    blk = cm_ref.shape[-1]
    for g, win in enumerate(POOL_WINDOWS):
        half = win // 2
        cols = slice(g * gc, (g + 1) * gc)
        rs = None
        for o in range(-half, half):
            part = h_ref[pl.ds(POOL_HALO + o * GRID_W, POOL_TILE), cols]
            rs = part if rs is None else rs + part
        segs = []
        for s in range(POOL_TILE // blk):
            hi, lo = _split_bf16(rs[s * blk:(s + 1) * blk])
            segs.append(_dot(cm_ref[g], hi) + _dot(cm_ref[g], lo))
        box = jnp.concatenate(segs, axis=0)
        cnt = ((jnp.minimum(r + half, n_rows) - jnp.maximum(r - half, 0))
               * (jnp.minimum(c + half, GRID_W) - jnp.maximum(c - half, 0))).astype(F32)
        dd = box / cnt - h_ref[POOL_HALO:POOL_HALO + POOL_TILE, cols]
        y = _pool_group_out(dd, whi_ref, wlo_ref, g)
        o_ref[:, cols] = x_ref[:, cols] + gt[:, cols] * ((y + bp_ref[:, cols]) * ps_ref[:, cols])


def _pool_ctx_kernel(x_ref, mod_ref, nw_ref, whi_ref, wlo_ref, bp_ref, ps_ref, cm_ref, prev_ref, o_ref, *, d, ctx_row):
    del prev_ref
    gc = d // len(POOL_WINDOWS)
    length = x_ref.shape[0]
    sh = mod_ref[ctx_row:ctx_row + 1, 0:d]
    sc = mod_ref[ctx_row:ctx_row + 1, d:2 * d]
    gt = mod_ref[ctx_row:ctx_row + 1, 2 * d:3 * d]
    x = x_ref[...]
    h = _norm_mod(x, nw_ref[...], sc, sh)
    t = lax.broadcasted_iota(jnp.int32, (length, 1), 0)
    for g, win in enumerate(POOL_WINDOWS):
        half = win // 2
        cols = slice(g * gc, (g + 1) * gc)
        hg = h[:, cols]
        hi, lo = _split_bf16(hg)
        box = _dot(cm_ref[g], hi) + _dot(cm_ref[g], lo)
        cnt = (jnp.minimum(t + half, length) - jnp.maximum(t - half, 0)).astype(F32)
        dd = box / cnt - hg
        y = _pool_group_out(dd, whi_ref, wlo_ref, g)
        o_ref[:, cols] = x[:, cols] + gt[:, cols] * ((y + bp_ref[:, cols]) * ps_ref[:, cols])


def _band_matrices(n, period):
    t = np.arange(n)
    out = []
    for win in POOL_WINDOWS:
        half = win // 2
        diff = t[None, :] - t[:, None]
        same = (t[None, :] // period) == (t[:, None] // period)
        out.append(((diff >= -half) & (diff < half) & same).astype(np.float32))
    return jnp.asarray(np.stack(out), dtype=BF16)


def _pool_layer(x, ctx, mod, nw, w_pool, b_pool, pool_scale):
    bsz, n, d = x.shape
    c_len = ctx.shape[1]
    n_groups = len(POOL_WINDOWS)
    gc = d // n_groups
    t_lat, t_ctx = bsz * n, bsz * c_len
    whi, wlo = _split_bf16(w_pool)
    nw2, bp2, ps2 = nw.reshape(1, d), b_pool.reshape(1, d), pool_scale.reshape(1, d)
    nt = n // POOL_TILE
    nh = n // POOL_HALO
    ratio = POOL_TILE // POOL_HALO
    const2 = lambda b, i: (0, 0)
    const3 = lambda b, i: (0, 0, 0)
    lat = pl.pallas_call(
        functools.partial(_pool_lat_kernel, d=d, n_rows=n // GRID_W),
        out_shape=jax.ShapeDtypeStruct((t_lat + t_ctx, d), F32),
        grid=(bsz, nt),
        in_specs=[
            pl.BlockSpec((None, POOL_TILE, d), lambda b, i: (b, i, 0)),
            pl.BlockSpec((None, POOL_HALO, d), lambda b, i: (b, jnp.maximum(ratio * i - 1, 0), 0)),
            pl.BlockSpec((None, POOL_HALO, d), lambda b, i: (b, jnp.minimum(ratio * i + ratio, nh - 1), 0)),
            pl.BlockSpec(mod.shape, const2),
            pl.BlockSpec((1, d), const2),
            pl.BlockSpec((n_groups, gc, gc), const3),
            pl.BlockSpec((n_groups, gc, gc), const3),
            pl.BlockSpec((1, d), const2),
            pl.BlockSpec((1, d), const2),
            pl.BlockSpec((n_groups, 256, 256), const3),
            pl.BlockSpec(memory_space=pl.ANY),
        ],
        out_specs=pl.BlockSpec((POOL_TILE, d), lambda b, i: (b * nt + i, 0)),
        scratch_shapes=[pltpu.VMEM((POOL_TILE + 2 * POOL_HALO, d), F32)],
        input_output_aliases={10: 0},
        compiler_params=_params("arbitrary", "arbitrary"),
        name="pool_lat",
    )(x, x, x, mod, nw2, whi, wlo, bp2, ps2, _band_matrices(256, GRID_W), jnp.zeros((t_lat + t_ctx, d), F32))
    const1 = lambda b: (0, 0)
    const13 = lambda b: (0, 0, 0)
    return pl.pallas_call(
        functools.partial(_pool_ctx_kernel, d=d, ctx_row=bsz),
        out_shape=jax.ShapeDtypeStruct((t_lat + t_ctx, d), F32),
        grid=(bsz,),
        in_specs=[
            pl.BlockSpec((None, c_len, d), lambda b: (b, 0, 0)),
            pl.BlockSpec(mod.shape, const1),
            pl.BlockSpec((1, d), const1),
            pl.BlockSpec((n_groups, gc, gc), const13),
            pl.BlockSpec((n_groups, gc, gc), const13),
            pl.BlockSpec((1, d), const1),
            pl.BlockSpec((1, d), const1),
            pl.BlockSpec((n_groups, c_len, c_len), const13),
            pl.BlockSpec(memory_space=pl.ANY),
        ],
        out_specs=pl.BlockSpec((c_len, d), lambda b: (t_lat // c_len + b, 0)),
        input_output_aliases={8: 0},
        compiler_params=_params("arbitrary"),
        name="pool_ctx",
    )(ctx, mod, nw2, whi, wlo, bp2, ps2, _band_matrices(c_len, c_len), lat)


def _for_rows(n, fn):
    def body(step, carry):
        base = pl.multiple_of(step * ROW_UNROLL, ROW_UNROLL)
        for u in range(ROW_UNROLL):
            fn(base + u)
        return carry

    lax.fori_loop(0, n // ROW_UNROLL, body, 0)


def _for_count(n, fn):
    def body(j, carry):
        fn(j)
        return carry

    lax.fori_loop(0, n, body, 0)


def _group(ref, g):
    return ref.at[pl.ds(pl.multiple_of(g * GROUP, GROUP), GROUP)]


def _mod_row(tile, lat_tiles, tiles_per_batch, ctx_row):
    return jnp.where(tile < lat_tiles, tile // tiles_per_batch, ctx_row)


def _route_kernel(x_ref, mod_ref, nw_ref, wrh_ref, wrl_ref, br_ref, ls_ref, up_ref,
                  rtok_ref, rt_ref, ng_ref, cb_ref, cnt_ref, carry_ref, *, d, lat_tiles, tiles_per_batch, ctx_row):
    i = pl.program_id(0)
    row = _mod_row(i, lat_tiles, tiles_per_batch, ctx_row)
    sh = mod_ref[pl.ds(row, 1), 3 * d:4 * d]
    sc = mod_ref[pl.ds(row, 1), 4 * d:5 * d]
    h = _norm_mod(x_ref[...], nw_ref[...], sc, sh)
    logits = _dot3(h, wrh_ref[...], wrl_ref[...]) + br_ref[...]
    lane = lax.broadcasted_iota(jnp.int32, logits.shape, 1).astype(F32)
    neg = jnp.float32(-jnp.inf)
    big = jnp.float32(1e9)
    is_g = lane < N_GROUPS
    lg = jnp.where(is_g, logits, neg)
    gmax = jnp.max(lg, axis=1, keepdims=True)
    g_idx = jnp.min(jnp.where(lg == gmax, lane, big), axis=1, keepdims=True)
    pg_top = 1.0 / jnp.sum(jnp.where(is_g, jnp.exp(lg - gmax), 0.0), axis=1, keepdims=True)
    lo = N_GROUPS + EXPERTS_PER_GROUP * g_idx
    le = jnp.where((lane >= lo) & (lane < lo + EXPERTS_PER_GROUP), logits, neg)
    e1 = jnp.max(le, axis=1, keepdims=True)
    i1 = jnp.min(jnp.where(le == e1, lane, big), axis=1, keepdims=True)
    le2 = jnp.where(lane == i1, neg, le)
    e2 = jnp.max(le2, axis=1, keepdims=True)
    i2 = jnp.min(jnp.where(le2 == e2, lane, big), axis=1, keepdims=True)
    r21 = jnp.exp(e2 - e1)
    w1 = pg_top / (1.0 + r21)
    w2 = pg_top * r21 / (1.0 + r21)
    id1 = i1 - N_GROUPS
    id2 = i2 - N_GROUPS

    @pl.when(i == 0)
    def _():
        carry_ref[...] = jnp.zeros_like(carry_ref)

    oh1 = (lane == id1).astype(F32)
    oh2 = (lane == id2).astype(F32)
    cum1 = _dot(ls_ref[...], oh1.astype(BF16))
    cum2 = _dot(ls_ref[...], oh2.astype(BF16))
    tot1 = jnp.sum(oh1, axis=0, keepdims=True)
    tot2 = jnp.sum(oh2, axis=0, keepdims=True)
    ng = jnp.floor((tot1 + tot2 + (GROUP - 1.0)) * (1.0 / GROUP))
    ng8 = jnp.broadcast_to(ng, (SUBLANES, ng.shape[1])).astype(BF16)
    lstart = GROUP * _dot(ng8, up_ref[...])[0:1]
    lpos1 = jnp.sum(oh1 * (cum1 + lstart), axis=1, keepdims=True)
    lpos2 = jnp.sum(oh2 * (cum2 + tot1 + lstart), axis=1, keepdims=True)
    carry = carry_ref[...]
    ng_ref[0] = ng
    cb_ref[0] = carry
    carry = carry + ng
    carry_ref[...] = carry
    cnt_ref[...] = carry

    fields = (id1, id2, lpos1, lpos2, w1, w2)
    slab = jnp.zeros_like(logits)
    for k, f in enumerate(fields):
        slab = jnp.where(lane == k, f, slab)
    rtok_ref[...] = slab
    rt_ref[...] = slab.T[0:SUBLANES]


def _dispatch_kernel(zl_ref, tab_ref, x_ref, mod_ref, nw_ref, *rest, d, lat_tiles, tiles_per_batch, ctx_row, n_zero_max,
                     zero_fill):
    o_ref, h_ref, l_ref, z_ref, cnt_ref, sem, zsem = rest[-7:]
    i = pl.program_id(0)
    tab_n = tab_ref.shape[-1]
    slot = i % 2

    @pl.when(i == 0)
    def _():
        l_ref[...] = jnp.zeros_like(l_ref)
        z_ref[...] = jnp.zeros_like(z_ref)

        def zcopy(z):
            return pltpu.make_async_copy(z_ref, _group(o_ref, zl_ref[z]), zsem)

        if zero_fill:
            _for_count(zl_ref[n_zero_max], lambda z: zcopy(z).start())
            _for_count(zl_ref[n_zero_max], lambda z: zcopy(z).wait())

    row = _mod_row(i, lat_tiles, tiles_per_batch, ctx_row)
    sh = mod_ref[pl.ds(row, 1), 3 * d:4 * d]
    sc = mod_ref[pl.ds(row, 1), 4 * d:5 * d]
    h_ref[...] = _norm_mod(x_ref[...], nw_ref[...], sc, sh).reshape(h_ref.shape)

    def place(r):
        v = h_ref[r]
        l_ref[slot, tab_ref[0, 0, r]] = v
        l_ref[slot, tab_ref[0, 0, TOK_TILE + r]] = v

    _for_rows(TOK_TILE, place)

    def gcopy(s, j, dst):
        return pltpu.make_async_copy(_group(l_ref.at[s], j), _group(o_ref, dst), sem.at[s])

    n_groups = tab_ref[0, 0, tab_n - 1]
    _for_count(n_groups, lambda j: gcopy(slot, j, tab_ref[0, 0, 2 * TOK_TILE + j]).start())

    @pl.when(i > 0)
    def _():
        _for_count(cnt_ref[0], lambda j: gcopy(1 - slot, j, 0).wait())

    cnt_ref[0] = n_groups

    @pl.when(i == pl.num_programs(0) - 1)
    def _():
        _for_count(n_groups, lambda j: gcopy(slot, j, 0).wait())


def _gmm_kernel(te_ref, nu_ref, x_ref, wg_ref, wu_ref, wd_ref, o_ref, wgb_ref, wub_ref, wdb_ref):
    i = pl.program_id(0)

    @pl.when((i == 0) | (te_ref[i] != te_ref[jnp.maximum(i - 1, 0)]))
    def _():
        wgb_ref[...] = wg_ref[...].astype(BF16)
        wub_ref[...] = wu_ref[...].astype(BF16)
        wdb_ref[...] = wd_ref[...].astype(BF16)

    @pl.when(i < nu_ref[0])
    def _():
        x = x_ref[...].reshape(x_ref.shape[0], wgb_ref.shape[0]).astype(BF16)
        hid = _silu(_dot(x, wgb_ref[...])) * _dot(x, wub_ref[...])
        o_ref[...] = _dot(hid.astype(BF16), wdb_ref[...]).reshape(o_ref.shape)

    @pl.when(i >= nu_ref[0])
    def _():
        o_ref[...] = jnp.zeros_like(o_ref)


def _combine_kernel(tab_ref, nxt_ref, x_ref, rtok_ref, mod_ref, nf_ref, y_ref, o_ref, l_ref, g1_ref, g2_ref, sem,
                    *, d, lat_tiles, tiles_per_batch, ctx_row, final_norm):
    i = pl.program_id(0)
    tab_n = tab_ref.shape[-1]
    slot = i % 2
    row = _mod_row(i, lat_tiles, tiles_per_batch, ctx_row)
    gt = mod_ref[pl.ds(row, 1), 5 * d:6 * d]

    def gcopy(s, j, src):
        return pltpu.make_async_copy(_group(y_ref, src), _group(l_ref.at[s], j), sem.at[s])

    @pl.when(i == 0)
    def _():
        _for_count(tab_ref[0, 0, tab_n - 1], lambda j: gcopy(0, j, tab_ref[0, 0, 2 * TOK_TILE + j]).start())

    @pl.when(i < pl.num_programs(0) - 1)
    def _():
        _for_count(nxt_ref[0, 0, tab_n - 1], lambda j: gcopy(1 - slot, j, nxt_ref[0, 0, 2 * TOK_TILE + j]).start())

    _for_count(tab_ref[0, 0, tab_n - 1], lambda j: gcopy(slot, j, 0).wait())

    def pick(r):
        g1_ref[r] = l_ref[slot, tab_ref[0, 0, r]]
        g2_ref[r] = l_ref[slot, tab_ref[0, 0, TOK_TILE + r]]

    _for_rows(TOK_TILE, pick)
    rt = rtok_ref[...]
    w1 = rt[:, 4:5]
    w2 = rt[:, 5:6]
    out = x_ref[...] + gt * (w1 * g1_ref[...].reshape(x_ref.shape) + w2 * g2_ref[...].reshape(x_ref.shape))
    if final_norm:
        out = out * lax.rsqrt(jnp.mean(out * out, axis=-1, keepdims=True) + EPS) * nf_ref[...]
    o_ref[...] = out


def _moe_layer(xflat, n_tok, mod, nw, w_rg, b_rg, w_re, b_re, wg, wu, wd, layer, *, t_lat, n, ctx_row, norm_final,
               n_tok_max, sorted_prev=None):
    d = xflat.shape[1]
    n_exp = w_re.shape[1]
    tiles = n_tok // TOK_TILE
    lat_tiles = t_lat // TOK_TILE
    tiles_per_batch = n // TOK_TILE
    common = dict(d=d, lat_tiles=lat_tiles, tiles_per_batch=tiles_per_batch, ctx_row=ctx_row)
    nw2 = nw.reshape(1, d)
    wr = jnp.zeros((d, LANES), F32).at[:, :N_GROUPS].set(w_rg).at[:, N_GROUPS:N_GROUPS + n_exp].set(w_re)
    br = jnp.zeros((1, LANES), F32).at[0, :N_GROUPS].set(b_rg).at[0, N_GROUPS:N_GROUPS + n_exp].set(b_re)
    wrh, wrl = _split_bf16(wr)
    lstrict = jnp.asarray(np.tril(np.ones((TOK_TILE, TOK_TILE), np.float32), -1), dtype=BF16)
    upper = jnp.asarray(np.triu(np.ones((LANES, LANES), np.float32), 1), dtype=BF16)
    const = lambda i: (0, 0)
    per_tile = pl.BlockSpec((1, 1, LANES), lambda i: (i, 0, 0))
    rtok, rt, ngf, cbf, cnt = pl.pallas_call(
        functools.partial(_route_kernel, **common),
        out_shape=(jax.ShapeDtypeStruct((n_tok, LANES), F32),
                   jax.ShapeDtypeStruct((SUBLANES, n_tok), F32),
                   jax.ShapeDtypeStruct((tiles, 1, LANES), F32),
                   jax.ShapeDtypeStruct((tiles, 1, LANES), F32),
                   jax.ShapeDtypeStruct((1, LANES), F32)),
        grid=(tiles,),
        in_specs=[
            pl.BlockSpec((TOK_TILE, d), lambda i: (i, 0)),
            pl.BlockSpec(mod.shape, const),
            pl.BlockSpec((1, d), const),
            pl.BlockSpec((d, LANES), const),
            pl.BlockSpec((d, LANES), const),
            pl.BlockSpec((1, LANES), const),
            pl.BlockSpec((TOK_TILE, TOK_TILE), const),
            pl.BlockSpec((LANES, LANES), const),
        ],
        out_specs=(pl.BlockSpec((TOK_TILE, LANES), lambda i: (i, 0)),
                   pl.BlockSpec((SUBLANES, TOK_TILE), lambda i: (0, i)),
                   per_tile, per_tile,
                   pl.BlockSpec((1, LANES), const)),
        scratch_shapes=[pltpu.VMEM((1, LANES), F32)],
        compiler_params=_params("arbitrary"),
        name="moe_route",
    )(xflat, mod, nw2, wrh, wrl, br, lstrict, upper)

    i32 = jnp.int32
    experts = jnp.arange(n_exp, dtype=i32)
    ng_t = ngf[:, 0, :n_exp].astype(i32)
    cb_t = cbf[:, 0, :n_exp].astype(i32)
    tot_g = cnt[0, :n_exp].astype(i32)
    gpt = MOE_TILE // GROUP
    local_groups = (2 * TOK_TILE) // GROUP + n_exp
    n_sorted_tiles = pl.cdiv((2 * n_tok_max) // GROUP + n_exp * (n_tok_max // TOK_TILE), gpt) + n_exp
    n_sorted_groups = n_sorted_tiles * gpt
    padded_g = ((tot_g + gpt - 1) // gpt) * gpt
    ends_g = jnp.cumsum(padded_g)
    off_g = ends_g - padded_g
    n_used = (ends_g[-1] // gpt).astype(i32).reshape(1)
    tile_ids = jnp.arange(n_sorted_tiles, dtype=i32)
    tile_expert = jnp.sum(tile_ids[:, None] * gpt >= ends_g[None, :], axis=1).astype(i32)
    last_expert = jnp.sum(jnp.maximum(n_used - 1, 0) * gpt >= ends_g).astype(i32)
    tile_expert = jnp.minimum(tile_expert, last_expert)
    lend = jnp.cumsum(ng_t, axis=1)
    jg = jnp.arange(local_groups, dtype=i32)
    seg = jnp.sum(jg[None, :, None] >= lend[:, None, :], axis=-1).astype(i32)
    base = off_g[None, :] + cb_t - (lend - ng_t)
    dst = jnp.sum(jnp.where(seg[:, :, None] == experts, base[:, None, :], 0), axis=-1) + jg[None, :]
    dst = jnp.where(seg < n_exp, dst, 0)
    lpos = rt[2:4].astype(i32).reshape(2, tiles, TOK_TILE).transpose(1, 0, 2).reshape(tiles, 2 * TOK_TILE)
    tab_n = 2 * TOK_TILE + pl.cdiv(local_groups + 1, LANES) * LANES
    fill = jnp.zeros((tiles, tab_n - 2 * TOK_TILE - local_groups - 1), i32)
    tab = jnp.concatenate([lpos, dst.astype(i32), fill, lend[:, -1:]], axis=1).reshape(tiles, 1, tab_n)
    n_zero_max = n_sorted_groups - (2 * n_tok) // GROUP
    zc = jnp.concatenate([padded_g - tot_g, n_sorted_groups - ends_g[-1:]])
    zs = jnp.concatenate([off_g + tot_g, ends_g[-1:]])
    zend = jnp.cumsum(zc)
    zi = jnp.arange(n_zero_max, dtype=i32)
    zseg = jnp.sum(zi[:, None] >= zend[None, :], axis=-1).astype(i32)
    zl = jnp.sum(jnp.where(zseg[:, None] == jnp.arange(n_exp + 1, dtype=i32), (zs - (zend - zc))[None, :], 0), axis=-1) + zi
    zlist = jnp.concatenate([jnp.where(zi < zend[-1], zl, 0), zend[-1:]]).astype(i32)

    n_sorted = n_sorted_groups * GROUP
    row = (d // LANES, LANES)
    local_rows = local_groups * GROUP
    reuse = sorted_prev is not None
    xs_sorted = pl.pallas_call(
        functools.partial(_dispatch_kernel, n_zero_max=n_zero_max, zero_fill=not reuse, **common),
        out_shape=jax.ShapeDtypeStruct((n_sorted,) + row, F32),
        grid_spec=pltpu.PrefetchScalarGridSpec(
            num_scalar_prefetch=1,
            grid=(tiles,),
            in_specs=[
                pl.BlockSpec((1, 1, tab_n), lambda i, zl: (i, 0, 0), memory_space=pltpu.SMEM),
                pl.BlockSpec((TOK_TILE, d), lambda i, zl: (i, 0)),
                pl.BlockSpec(mod.shape, lambda i, zl: (0, 0)),
                pl.BlockSpec((1, d), lambda i, zl: (0, 0)),
            ] + ([pl.BlockSpec(memory_space=pl.ANY)] if reuse else []),
            out_specs=pl.BlockSpec(memory_space=pl.ANY),
            scratch_shapes=[pltpu.VMEM((TOK_TILE,) + row, F32), pltpu.VMEM((2, local_rows) + row, F32),
                            pltpu.VMEM((GROUP,) + row, F32), pltpu.SMEM((1,), jnp.int32),
                            pltpu.SemaphoreType.DMA((2,)), pltpu.SemaphoreType.DMA(())],
        ),
        input_output_aliases={5: 0} if reuse else {},
        compiler_params=_params("arbitrary"),
        name="moe_dispatch",
    )(zlist, tab, xflat, mod, nw2, *([sorted_prev] if reuse else []))

    de = wg.shape[-1]
    y_sorted = pl.pallas_call(
        _gmm_kernel,
        out_shape=jax.ShapeDtypeStruct((n_sorted,) + row, F32),
        grid_spec=pltpu.PrefetchScalarGridSpec(
            num_scalar_prefetch=2,
            grid=(n_sorted_tiles,),
            in_specs=[
                pl.BlockSpec((MOE_TILE,) + row, lambda i, te, nu: (jnp.minimum(i, jnp.maximum(nu[0] - 1, 0)), 0, 0)),
                pl.BlockSpec((None, None, d, de), lambda i, te, nu: (layer, te[i], 0, 0)),
                pl.BlockSpec((None, None, d, de), lambda i, te, nu: (layer, te[i], 0, 0)),
                pl.BlockSpec((None, None, de, d), lambda i, te, nu: (layer, te[i], 0, 0)),
            ],
            out_specs=pl.BlockSpec((MOE_TILE,) + row, lambda i, te, nu: (i, 0, 0)),
            scratch_shapes=[pltpu.VMEM((d, de), BF16), pltpu.VMEM((d, de), BF16), pltpu.VMEM((de, d), BF16)],
        ),
        compiler_params=_params("arbitrary"),
        name="moe_gmm",
    )(tile_expert, n_used, xs_sorted, wg, wu, wd)

    final = norm_final is not None
    nf = (norm_final if final else jnp.ones((d,), F32)).reshape(1, d)
    out = pl.pallas_call(
        functools.partial(_combine_kernel, final_norm=final, **common),
        out_shape=jax.ShapeDtypeStruct((n_tok, d), F32),
        grid=(tiles,),
        in_specs=[
            pl.BlockSpec((1, 1, tab_n), lambda i: (i, 0, 0), memory_space=pltpu.SMEM),
            pl.BlockSpec((1, 1, tab_n), lambda i: (jnp.minimum(i + 1, tiles - 1), 0, 0), memory_space=pltpu.SMEM),
            pl.BlockSpec((TOK_TILE, d), lambda i: (i, 0)),
            pl.BlockSpec((TOK_TILE, LANES), lambda i: (i, 0)),
            pl.BlockSpec(mod.shape, const),
            pl.BlockSpec((1, d), const),
            pl.BlockSpec(memory_space=pl.ANY),
        ],
        out_specs=pl.BlockSpec((TOK_TILE, d), lambda i: (i, 0)),
        scratch_shapes=[pltpu.VMEM((2, local_rows) + row, F32), pltpu.VMEM((TOK_TILE,) + row, F32),
                        pltpu.VMEM((TOK_TILE,) + row, F32), pltpu.SemaphoreType.DMA((2,))],
        compiler_params=_params("arbitrary"),
        name="moe_combine",
    )(tab, tab, xflat, rtok, mod, nf, y_sorted)
    return out, xs_sorted


def _dn_proj_kernel(x_ref, xp_ref, xn_ref, mod_ref, nw_ref, wqkv_ref, wz_ref, wabh_ref, wabl_ref, wconv_ref, gpar_ref,
                    tril_ref, q_ref, k_ref, v_ref, z_ref, gct_ref, gcr_ref, hf_ref,
                    *, d, tile, tiles_per_seq, lat, ctx_row):
    i = pl.program_id(0)
    row = (i // tiles_per_seq) if lat else ctx_row
    sh = mod_ref[pl.ds(row, 1), 0:d]
    sc = mod_ref[pl.ds(row, 1), d:2 * d]
    nw = nw_ref[...]
    first = (i % tiles_per_seq) == 0
    last = (i % tiles_per_seq) == tiles_per_seq - 1
    halo = SUBLANES
    hf_ref[0:halo] = jnp.where(first, 0.0, _norm_mod(xp_ref[...], nw, sc, sh))
    hm = _norm_mod(x_ref[...], nw, sc, sh)
    hf_ref[halo:halo + tile] = hm
    hf_ref[halo + tile:] = jnp.where(last, 0.0, _norm_mod(xn_ref[...], nw, sc, sh))
    hb = hf_ref[...].astype(BF16)

    z_ref[...] = _dot(hm.astype(BF16), wz_ref[...]).astype(z_ref.dtype)

    ab = _dot3(hm, wabh_ref[...], wabl_ref[...])
    a_log, dt_bias, is_g, is_f, is_b = (gpar_ref[j:j + 1, :] for j in range(5))
    xg = ab + dt_bias
    softplus = jnp.maximum(xg, 0.0) + jnp.log(1.0 + jnp.exp(-jnp.abs(xg)))
    slab = jnp.where(is_g > 0.5, -jnp.exp(a_log) * softplus, jax.nn.sigmoid(ab))
    s_hi, s_lo = _split_bf16(slab)
    pre, tot = [], []
    for cc in range(tile // DN_CHUNK):
        rows = slice(cc * DN_CHUNK, (cc + 1) * DN_CHUNK)
        p = _dot(tril_ref[...], s_hi[rows]) + _dot(tril_ref[...], s_lo[rows])
        pre.append(p)
        tot.append(jnp.broadcast_to(p[DN_CHUNK - 1:DN_CHUNK], p.shape))
    gpre = jnp.concatenate(pre, axis=0)
    gsuf = jnp.concatenate(tot, axis=0) - gpre + slab
    gcs = jnp.where(is_f > 0.5, gpre, jnp.where(is_b > 0.5, gsuf, slab))
    gct_ref[...] = gcs
    gcs_t = gcs.T
    n_ab = 4 * DN_HEADS
    for cc in range(tile // DN_CHUNK):
        gcr_ref[cc] = gcs_t[0:n_ab, cc * DN_CHUNK:(cc + 1) * DN_CHUNK]

    dk = d // DN_HEADS
    cw = 4 * dk
    outs = (q_ref, k_ref, v_ref)
    for cc in range(3 * d // cw):
        cols = slice(cc * cw, (cc + 1) * cw)
        pc = _dot(hb, wqkv_ref[:, cols])
        wc = wconv_ref[:, cols]
        n_rows = tile + 2 * halo
        conv = (pltpu.roll(pc, 2, 0)[halo:halo + tile] * wc[0:1]
                + pltpu.roll(pc, 1, 0)[halo:halo + tile] * wc[1:2]
                + pc[halo:halo + tile] * wc[2:3]
                + pltpu.roll(pc, n_rows - 1, 0)[halo:halo + tile] * wc[3:4])
        act = _silu(conv)
        which = (cc * cw) // d
        for hh in range(cw // dk):
            head = ((cc * cw) % d) // dk + hh
            t = act[:, hh * dk:(hh + 1) * dk]
            if which < 2:
                t = t * lax.rsqrt(jnp.sum(t * t, axis=-1, keepdims=True) + EPS)
            if which == 0:
                t = t * (dk ** -0.5)
            outs[which][head] = t


def _dn_proj(xflat, tok_off, n_seq, seq_len, tile, lat, mod, nw, wqkv, wz, wab, wconv, gpar, ctx_row):
    d = xflat.shape[1]
    dk = d // DN_HEADS
    t_n = n_seq * seq_len
    tiles = t_n // tile
    tiles_per_seq = seq_len // tile
    boff = tok_off // tile
    hoff = tok_off // SUBLANES
    hper = tile // SUBLANES
    hmax = xflat.shape[0] // SUBLANES - 1
    tril = jnp.asarray(np.tril(np.ones((DN_CHUNK, DN_CHUNK), np.float32)), dtype=BF16)
    wabh, wabl = _split_bf16(wab)
    const = lambda i: (0, 0)
    hshape = jax.ShapeDtypeStruct((DN_HEADS, t_n, dk), F32)
    hspec = pl.BlockSpec((DN_HEADS, tile, dk), lambda i: (0, i, 0))
    n_ab = 4 * DN_HEADS
    return pl.pallas_call(
        functools.partial(_dn_proj_kernel, d=d, tile=tile, tiles_per_seq=tiles_per_seq, lat=lat, ctx_row=ctx_row),
        out_shape=(hshape, hshape, hshape,
                   jax.ShapeDtypeStruct((t_n, d), BF16),
                   jax.ShapeDtypeStruct((t_n, LANES), F32),
                   jax.ShapeDtypeStruct((t_n // DN_CHUNK, n_ab, DN_CHUNK), F32)),
        grid=(tiles,),
        in_specs=[
            pl.BlockSpec((tile, d), lambda i: (boff + i, 0)),
            pl.BlockSpec((SUBLANES, d), lambda i: (jnp.maximum(hoff + i * hper - 1, 0), 0)),
            pl.BlockSpec((SUBLANES, d), lambda i: (jnp.minimum(hoff + (i + 1) * hper, hmax), 0)),
            pl.BlockSpec(mod.shape, const),
            pl.BlockSpec((1, d), const),
            pl.BlockSpec(wqkv.shape, const),
            pl.BlockSpec(wz.shape, const),
            pl.BlockSpec(wab.shape, const),
            pl.BlockSpec(wab.shape, const),
            pl.BlockSpec(wconv.shape, const),
            pl.BlockSpec(gpar.shape, const),
            pl.BlockSpec((DN_CHUNK, DN_CHUNK), const),
        ],
        out_specs=(hspec, hspec, hspec,
                   pl.BlockSpec((tile, d), lambda i: (i, 0)),
                   pl.BlockSpec((tile, LANES), lambda i: (i, 0)),
                   pl.BlockSpec((tile // DN_CHUNK, n_ab, DN_CHUNK), lambda i: (i, 0, 0))),
        scratch_shapes=[pltpu.VMEM((tile + 2 * SUBLANES, d), F32)],
        compiler_params=_params("arbitrary"),
        name="dn_proj_lat" if lat else "dn_proj_ctx",
    )(xflat, xflat, xflat, mod, nw.reshape(1, d), wqkv, wz, wabh, wabl, wconv, gpar, tril)


def _dn_chunk_kernel(q_ref, k_ref, v_ref, gct_ref, gcr_ref, s0_ref, *rest, blk, reverse, need_output):
    n_out = 2 if need_output else 1
    o_ref = rest[0] if need_output else None
    sf_ref, s_ref = rest[n_out - 1], rest[n_out]
    bufs = rest[n_out + 1:]
    sets = (bufs[:len(bufs) // 2], bufs[len(bufs) // 2:])
    j = pl.program_id(1)
    nchunk = blk // DN_CHUNK
    c = DN_CHUNK
    dv = v_ref.shape[-1]
    heads = range(DN_HEADS)

    @pl.when(j == 0)
    def _():
        s_ref[...] = s0_ref[...]

    ii = lax.broadcasted_iota(jnp.int32, (c, c), 0)
    jj = lax.broadcasted_iota(jnp.int32, (c, c), 1)
    strict = (ii < jj) if reverse else (ii > jj)
    incl = (ii <= jj) if reverse else (ii >= jj)
    field = 2 if reverse else 0
    nt_dims = (((1,), (1,)), ((), ()))

    def g_last(gcc):
        return gcc[0:1, :] if reverse else gcc[c - 1:c, :]

    def prep(pair, buf):
        u_s, wq_s, kdt_s = buf[:3]
        inst = []
        for cc in range(DN_PREP_CHUNKS):
            ci = pair * DN_PREP_CHUNKS + cc
            rows = slice(ci * c, (ci + 1) * c)
            gcols = gct_ref[rows, :]
            grows = gcr_ref[ci]
            for h in heads:
                ln = 4 * h + field
                inst.append((cc, rows, h, gcols[:, ln:ln + 1], gcols[:, ln + 1:ln + 2], grows[ln:ln + 1, :]))
        kc = [k_ref[h, rows, :] for _, rows, h, _, _, _ in inst]
        vc = [v_ref[h, rows, :] for _, rows, h, _, _, _ in inst]
        decay = [jnp.exp(jnp.where(incl, gcc - gcr, 0.0)) for _, _, _, gcc, _, gcr in inst]
        egc = [jnp.exp(t[3]) for t in inst]
        kb = [kc_i * t[4] for kc_i, t in zip(kc, inst)]
        if need_output:
            qc = [q_ref[h, rows, :] for _, rows, h, _, _, _ in inst]
            lhs = [jnp.concatenate([kb_i, qc_i], axis=0).astype(BF16) for kb_i, qc_i in zip(kb, qc)]
        else:
            lhs = [kb_i.astype(BF16) for kb_i in kb]
        kq = [lax.dot_general(l_i, kc_i.astype(BF16), nt_dims, preferred_element_type=F32)
              for l_i, kc_i in zip(lhs, kc)]
        e = [jnp.where(strict, -(kq_i[0:c] * d_i), 0.0) for kq_i, d_i in zip(kq, decay)]
        pb = [e_i.astype(BF16) for e_i in e]
        p = [_dot(p_i, p_i) for p_i in pb]
        for level in range(1, 6):
            pb = [p_i.astype(BF16) for p_i in p]
            if level < 5:
                out = [_dot(jnp.concatenate([e_i.astype(BF16), pb_i], axis=0), pb_i) for pb_i, e_i in zip(pb, e)]
                e = [e_i + p_i + o_i[0:c] for e_i, p_i, o_i in zip(e, p, out)]
                p = [o_i[c:2 * c] for o_i in out]
            else:
                e = [e_i + p_i + _dot(e_i.astype(BF16), pb_i) for e_i, p_i, pb_i in zip(e, p, pb)]
        rhs = [jnp.concatenate([vc_i * t[4], kb_i * e_i], axis=1) for vc_i, kb_i, e_i, t in zip(vc, kb, egc, inst)]
        x = [r_i + _dot(e_i.astype(BF16), r_i.astype(BF16)) for e_i, r_i in zip(e, rhs)]
        for n_i, (cc, _, h, gcc, _, _) in enumerate(inst):
            loc = slice(cc * c, (cc + 1) * c)
            u_s[h, loc, :] = x[n_i][:, 0:dv]
            kdt_s[h, cc] = (kc[n_i] * jnp.exp(g_last(gcc) - gcc)).T.astype(BF16)
            if need_output:
                wq_s[h, cc] = jnp.concatenate([x[n_i][:, dv:], qc[n_i] * egc[n_i]], axis=0).astype(BF16)
                buf[3][h, loc, :] = jnp.where(incl, kq[n_i][c:2 * c] * decay[n_i], 0.0).astype(BF16)
            else:
                wq_s[h, cc] = x[n_i][:, dv:].astype(BF16)

    def recur(pair, cc, buf):
        u_s, wq_s, kdt_s = buf[:3]
        ci = pair * DN_PREP_CHUNKS + cc
        rows = slice(ci * c, (ci + 1) * c)
        loc = slice(cc * c, (cc + 1) * c)
        gcols = gct_ref[rows, :]
        s = [s_ref[h] for h in heads]
        sb = [s_h.astype(BF16) for s_h in s]
        ws = [_dot(wq_s[h, cc], sb[h]) for h in heads]
        vb = [(u_s[h, loc, :] - ws[h][0:c]).astype(BF16) for h in heads]
        if need_output:
            for h in heads:
                o_ref[h, rows, :] = (ws[h][c:2 * c] + _dot(buf[3][h, loc, :], vb[h])).astype(o_ref.dtype)
        for h in heads:
            eg = jnp.exp(g_last(gcols[:, 4 * h + field:4 * h + field + 1]))
            s_ref[h] = s[h] * eg + _dot(kdt_s[h, cc], vb[h])

    pairs = list(range(nchunk // DN_PREP_CHUNKS))
    within = list(range(DN_PREP_CHUNKS))
    if reverse:
        pairs.reverse()
        within.reverse()
    prep(pairs[0], sets[0])
    for n_p, pair in enumerate(pairs):
        for cc in within:
            recur(pair, cc, sets[n_p % 2])
        if n_p + 1 < len(pairs):
            prep(pairs[n_p + 1], sets[(n_p + 1) % 2])

    @pl.when(j == pl.num_programs(1) - 1)
    def _():
        sf_ref[...] = s_ref[...]


def _dn_scan(q, k, v, gct, gcr, s0, n_seq, seq_len, blk, reverse, need_output):
    dk = q.shape[-1]
    nblk = seq_len // blk
    order = (lambda j: nblk - 1 - j) if reverse else (lambda j: j)
    hspec = pl.BlockSpec((DN_HEADS, blk, dk), lambda b, j: (0, b * nblk + order(j), 0))
    sspec = pl.BlockSpec((None, DN_HEADS, dk, dk), lambda b, j: (b, 0, 0, 0))
    n_ab = gcr.shape[1]
    s_shape = jax.ShapeDtypeStruct((n_seq, DN_HEADS, dk, dk), F32)
    out_shape = [s_shape]
    out_specs = [sspec]
    wq_rows = 2 * DN_CHUNK if need_output else DN_CHUNK
    pair_rows = DN_PREP_CHUNKS * DN_CHUNK
    buf_set = [pltpu.VMEM((DN_HEADS, pair_rows, dk), F32),
               pltpu.VMEM((DN_HEADS, DN_PREP_CHUNKS, wq_rows, dk), BF16),
               pltpu.VMEM((DN_HEADS, DN_PREP_CHUNKS, dk, DN_CHUNK), BF16)]
    if need_output:
        out_shape.insert(0, jax.ShapeDtypeStruct(q.shape, BF16))
        out_specs.insert(0, hspec)
        buf_set.append(pltpu.VMEM((DN_HEADS, pair_rows, DN_CHUNK), BF16))
    scratch = [pltpu.VMEM((DN_HEADS, dk, dk), F32)] + buf_set + buf_set
    res = pl.pallas_call(
        functools.partial(_dn_chunk_kernel, blk=blk, reverse=reverse, need_output=need_output),
        out_shape=tuple(out_shape),
        grid=(n_seq, nblk),
        in_specs=[
            hspec, hspec, hspec,
            pl.BlockSpec((blk, LANES), lambda b, j: (b * nblk + order(j), 0)),
            pl.BlockSpec((blk // DN_CHUNK, n_ab, DN_CHUNK), lambda b, j: (b * nblk + order(j), 0, 0)),
            sspec,
        ],
        out_specs=tuple(out_specs),
        scratch_shapes=scratch,
        compiler_params=_params("arbitrary", "arbitrary"),
        name="dn_scan_" + ("bwd" if reverse else "fwd") + ("_out" if need_output else "_state"),
    )(q, k, v, gct, gcr, s0)
    return (res[0], res[1]) if need_output else (None, res[0])


def _dn_out_kernel(of_ref, ob_ref, z_ref, x_ref, mod_ref, nrm_ref, wout_ref, o_ref, *, d, tiles_per_batch):
    i = pl.program_id(0)
    row = i // tiles_per_batch
    gt = mod_ref[pl.ds(row, 1), 2 * d:3 * d]
    dk = d // DN_HEADS
    parts = []
    for h in range(DN_HEADS):
        o = of_ref[h].astype(F32) + ob_ref[h].astype(F32)
        o = o * lax.rsqrt(jnp.mean(o * o, axis=-1, keepdims=True) + EPS) * nrm_ref[...]
        parts.append((o * _silu(z_ref[:, h * dk:(h + 1) * dk].astype(F32))).astype(BF16))
    y = _dot(jnp.concatenate(parts, axis=1), wout_ref[...])
    o_ref[...] = x_ref[...] + gt * y


def _deltanet_layer(xflat, bsz, n, c_len, mod, nw, w_in, w_conv, a_log, dt_bias, dn_norm, w_out):
    d = xflat.shape[1]
    dk = d // DN_HEADS
    t_lat = bsz * n
    ctx_row = bsz
    wqkv = w_in[:, :3 * d].astype(BF16)
    wz = w_in[:, 3 * d:4 * d].astype(BF16)
    n_ab = 4 * DN_HEADS
    wab = w_in[:, 4 * d:].reshape(d, 4, DN_HEADS).transpose(0, 2, 1).reshape(d, n_ab)
    wab = jnp.zeros((d, LANES), F32).at[:, :n_ab].set(wab)
    lane = np.arange(LANES)
    fld, head, used = lane % 4, np.minimum(lane // 4, DN_HEADS - 1), lane < n_ab
    is_g = used & (fld % 2 == 0)
    dirn = fld // 2
    gpar = jnp.zeros((SUBLANES, LANES), F32)
    gpar = gpar.at[0].set(jnp.where(is_g, a_log[dirn, head], 0.0))
    gpar = gpar.at[1].set(jnp.where(is_g, dt_bias[dirn, head], 0.0))
    gpar = gpar.at[2].set(jnp.asarray(is_g, F32))
    gpar = gpar.at[3].set(jnp.asarray(used & (fld == 0), F32))
    gpar = gpar.at[4].set(jnp.asarray(used & (fld == 2), F32))

    proj = functools.partial(_dn_proj, xflat, mod=mod, nw=nw, wqkv=wqkv, wz=wz, wab=wab, wconv=w_conv, gpar=gpar,
                             ctx_row=ctx_row)
    q, k, v, z, gct, gcr = proj(tok_off=0, n_seq=bsz, seq_len=n, tile=TOK_TILE, lat=True)
    qc, kc, vc, _, gctc, gcrc = proj(tok_off=t_lat, n_seq=bsz, seq_len=c_len, tile=c_len, lat=False)

    s0 = jnp.zeros((bsz, DN_HEADS, dk, dk), F32)
    o_dir = []
    for reverse in (False, True):
        _, s_ctx = _dn_scan(qc, kc, vc, gctc, gcrc, s0, bsz, c_len, c_len, reverse, False)
        o, _ = _dn_scan(q, k, v, gct, gcr, s_ctx, bsz, n, TOK_TILE, reverse, True)
        o_dir.append(o)

    tiles = t_lat // TOK_TILE
    const = lambda i: (0, 0)
    hspec = pl.BlockSpec((DN_HEADS, TOK_TILE, dk), lambda i: (0, i, 0))
    return pl.pallas_call(
        functools.partial(_dn_out_kernel, d=d, tiles_per_batch=n // TOK_TILE),
        out_shape=jax.ShapeDtypeStruct((t_lat, d), F32),
        grid=(tiles,),
        in_specs=[
            hspec, hspec,
            pl.BlockSpec((TOK_TILE, d), lambda i: (i, 0)),
            pl.BlockSpec((TOK_TILE, d), lambda i: (i, 0)),
            pl.BlockSpec(mod.shape, const),
            pl.BlockSpec((1, dk), const),
            pl.BlockSpec((d, d), const),
        ],
        out_specs=pl.BlockSpec((TOK_TILE, d), lambda i: (i, 0)),
        compiler_params=_params("arbitrary"),
        name="dn_out",
    )(o_dir[0], o_dir[1], z, xflat, mod, dn_norm.reshape(1, dk), w_out.astype(BF16))


def kernel(x, c, ctx, c_ctx, w_ada, b_ada, norm_mix, norm_ffn, w_pool, b_pool, pool_scale, w_dn_in, w_dn_conv, dn_a_log, dn_dt_bias, dn_norm, w_dn_out, w_rg, b_rg, w_re, b_re, w_e_gate, w_e_up, w_e_down, norm_final):
    bsz, n, d = x.shape
    c_len = ctx.shape[1]
    assert w_ada.shape[0] == 2 and bsz < COND_ROWS and n % POOL_TILE == 0 and c_len == 256
    assert (bsz * c_len) % TOK_TILE == 0 and d % (4 * LANES) == 0
    t_lat = bsz * n
    t_all = t_lat + bsz * c_len
    ctx_row = bsz
    cond = jnp.zeros((COND_ROWS, d), F32).at[:bsz].set(c.astype(F32)).at[ctx_row].set(c_ctx.astype(F32))
    mod = _ada(cond, w_ada, b_ada)

    xflat = _pool_layer(x.astype(F32), ctx.astype(F32), mod[0], norm_mix[0], w_pool[0], b_pool[0], pool_scale[0])
    xflat, sorted0 = _moe_layer(xflat, t_all, mod[0], norm_ffn[0], w_rg[0], b_rg[0], w_re[0], b_re[0],
                                w_e_gate, w_e_up, w_e_down, 0, t_lat=t_lat, n=n, ctx_row=ctx_row, norm_final=None,
                                n_tok_max=t_all)

    xlat = _deltanet_layer(xflat, bsz, n, c_len, mod[1], norm_mix[1], w_dn_in[0], w_dn_conv[0], dn_a_log[0],
                           dn_dt_bias[0], dn_norm[0], w_dn_out[0])
    out, _ = _moe_layer(xlat, t_lat, mod[1], norm_ffn[1], w_rg[1], b_rg[1], w_re[1], b_re[1],
                        w_e_gate, w_e_up, w_e_down, 1, t_lat=t_lat, n=n, ctx_row=ctx_row, norm_final=norm_final,
                        n_tok_max=t_all, sorted_prev=sorted0)
    return out.reshape(bsz, n, d).astype(x.dtype)
```

```python
import functools

import numpy as np
import jax
import jax.numpy as jnp
from jax import lax
from jax.experimental import pallas as pl
from jax.experimental.pallas import tpu as pltpu

GRID_W = 64
POOL_WINDOWS = (2, 4, 8, 16)
DN_HEADS = 8
DN_CONV_W = 4
DN_CHUNK = 64
DN_PREP_CHUNKS = 2
N_GROUPS = 4
EXPERTS_PER_GROUP = 8
EPS = 1e-6

LANES = 128
SUBLANES = 8
VMEM_LIMIT_BYTES = 56 * 1024 * 1024

COND_ROWS = 8
TOK_TILE = 512
POOL_TILE = 1024
POOL_HALO = 512
MOE_TILE = 512
ROW_UNROLL = 8
GROUP = 8
BF16 = jnp.bfloat16
F32 = jnp.float32
HIGHEST = lax.Precision.HIGHEST


def _dot(a, b):
    return jnp.dot(a, b, preferred_element_type=F32)


def _dot_hi(a, b):
    return jnp.dot(a, b, precision=HIGHEST, preferred_element_type=F32)


def _split_bf16(a):
    hi = a.astype(BF16)
    lo = (a - hi.astype(F32)).astype(BF16)
    return hi, lo


def _dot3(a, b_hi, b_lo):
    a_hi, a_lo = _split_bf16(a)
    return _dot(a_hi, b_hi) + _dot(a_lo, b_hi) + _dot(a_hi, b_lo)


def _params(*sem):
    return pltpu.CompilerParams(dimension_semantics=sem, vmem_limit_bytes=VMEM_LIMIT_BYTES)


def _norm_mod(x, nw, sc, sh):
    return x * lax.rsqrt(jnp.mean(x * x, axis=-1, keepdims=True) + EPS) * nw * (1.0 + sc) + sh


def _silu(x):
    return x * jax.nn.sigmoid(x)


def _ada_kernel(cond_ref, w_ref, b_ref, o_ref):
    o_ref[...] = _dot_hi(_silu(cond_ref[...]), w_ref[...]) + b_ref[...]


def _ada(cond, w_ada, b_ada):
    depth, d, d6 = w_ada.shape
    tn = d6 // 4
    return pl.pallas_call(
        _ada_kernel,
        out_shape=jax.ShapeDtypeStruct((depth, COND_ROWS, d6), F32),
        grid=(depth, d6 // tn),
        in_specs=[
            pl.BlockSpec((COND_ROWS, d), lambda i, j: (0, 0)),
            pl.BlockSpec((None, d, tn), lambda i, j: (i, 0, j)),
            pl.BlockSpec((None, 1, tn), lambda i, j: (i, 0, j)),
        ],
        out_specs=pl.BlockSpec((None, COND_ROWS, tn), lambda i, j: (i, 0, j)),
        compiler_params=_params("arbitrary", "arbitrary"),
        name="ada",
    )(cond, w_ada, b_ada.reshape(depth, 1, d6))


def _pool_group_out(d, whi_ref, wlo_ref, g):
    dhi, dlo = _split_bf16(d)
    return _dot(dhi, whi_ref[g]) + _dot(dlo, whi_ref[g]) + _dot(dhi, wlo_ref[g])


def _pool_lat_kernel(x_ref, xp_ref, xn_ref, mod_ref, nw_ref, whi_ref, wlo_ref, bp_ref, ps_ref, cm_ref, zero_ref,
                     o_ref, h_ref, *, d, n_rows):
    del zero_ref
    b = pl.program_id(0)
    i = pl.program_id(1)
    nt = pl.num_programs(1)
    gc = d // len(POOL_WINDOWS)
    sh = mod_ref[pl.ds(b, 1), 0:d]
    sc = mod_ref[pl.ds(b, 1), d:2 * d]
    gt = mod_ref[pl.ds(b, 1), 2 * d:3 * d]
    nw = nw_ref[...]
    h_ref[0:POOL_HALO] = jnp.where(i > 0, _norm_mod(xp_ref[...], nw, sc, sh), 0.0)
    h_ref[POOL_HALO:POOL_HALO + POOL_TILE] = _norm_mod(x_ref[...], nw, sc, sh)
    h_ref[POOL_HALO + POOL_TILE:] = jnp.where(i < nt - 1, _norm_mod(xn_ref[...], nw, sc, sh), 0.0)

    tok = lax.broadcasted_iota(jnp.int32, (POOL_TILE, 1), 0)
    r = i * (POOL_TILE // GRID_W) + (tok >> 6)
    c = tok & (GRID_W - 1)
    blk = cm_ref.shape[-1]
    for g, win in enumerate(POOL_WINDOWS):
        half = win // 2
        cols = slice(g * gc, (g + 1) * gc)
        rs = None
        for o in range(-half, half):
            part = h_ref[pl.ds(POOL_HALO + o * GRID_W, POOL_TILE), cols]
            rs = part if rs is None else rs + part
        segs = []
        for s in range(POOL_TILE // blk):
            hi, lo = _split_bf16(rs[s * blk:(s + 1) * blk])
            segs.append(_dot(cm_ref[g], hi) + _dot(cm_ref[g], lo))
        box = jnp.concatenate(segs, axis=0)
        cnt = ((jnp.minimum(r + half, n_rows) - jnp.maximum(r - half, 0))
               * (jnp.minimum(c + half, GRID_W) - jnp.maximum(c - half, 0))).astype(F32)
        dd = box / cnt - h_ref[POOL_HALO:POOL_HALO + POOL_TILE, cols]
        y = _pool_group_out(dd, whi_ref, wlo_ref, g)
        o_ref[:, cols] = x_ref[:, cols] + gt[:, cols] * ((y + bp_ref[:, cols]) * ps_ref[:, cols])


def _pool_ctx_kernel(x_ref, mod_ref, nw_ref, whi_ref, wlo_ref, bp_ref, ps_ref, cm_ref, prev_ref, o_ref, *, d, ctx_row):
    del prev_ref
    gc = d // len(POOL_WINDOWS)
    length = x_ref.shape[0]
    sh = mod_ref[ctx_row:ctx_row + 1, 0:d]
    sc = mod_ref[ctx_row:ctx_row + 1, d:2 * d]
    gt = mod_ref[ctx_row:ctx_row + 1, 2 * d:3 * d]
    x = x_ref[...]
    h = _norm_mod(x, nw_ref[...], sc, sh)
    t = lax.broadcasted_iota(jnp.int32, (length, 1), 0)
    for g, win in enumerate(POOL_WINDOWS):
        half = win // 2
        cols = slice(g * gc, (g + 1) * gc)
        hg = h[:, cols]
        hi, lo = _split_bf16(hg)
        box = _dot(cm_ref[g], hi) + _dot(cm_ref[g], lo)
        cnt = (jnp.minimum(t + half, length) - jnp.maximum(t - half, 0)).astype(F32)
        dd = box / cnt - hg
        y = _pool_group_out(dd, whi_ref, wlo_ref, g)
        o_ref[:, cols] = x[:, cols] + gt[:, cols] * ((y + bp_ref[:, cols]) * ps_ref[:, cols])


def _band_matrices(n, period):
    t = np.arange(n)
    out = []
    for win in POOL_WINDOWS:
        half = win // 2
        diff = t[None, :] - t[:, None]
        same = (t[None, :] // period) == (t[:, None] // period)
        out.append(((diff >= -half) & (diff < half) & same).astype(np.float32))
    return jnp.asarray(np.stack(out), dtype=BF16)


def _pool_layer(x, ctx, mod, nw, w_pool, b_pool, pool_scale):
    bsz, n, d = x.shape
    c_len = ctx.shape[1]
    n_groups = len(POOL_WINDOWS)
    gc = d // n_groups
    t_lat, t_ctx = bsz * n, bsz * c_len
    whi, wlo = _split_bf16(w_pool)
    nw2, bp2, ps2 = nw.reshape(1, d), b_pool.reshape(1, d), pool_scale.reshape(1, d)
    nt = n // POOL_TILE
    nh = n // POOL_HALO
    ratio = POOL_TILE // POOL_HALO
    const2 = lambda b, i: (0, 0)
    const3 = lambda b, i: (0, 0, 0)
    lat = pl.pallas_call(
        functools.partial(_pool_lat_kernel, d=d, n_rows=n // GRID_W),
        out_shape=jax.ShapeDtypeStruct((t_lat + t_ctx, d), F32),
        grid=(bsz, nt),
        in_specs=[
            pl.BlockSpec((None, POOL_TILE, d), lambda b, i: (b, i, 0)),
            pl.BlockSpec((None, POOL_HALO, d), lambda b, i: (b, jnp.maximum(ratio * i - 1, 0), 0)),
            pl.BlockSpec((None, POOL_HALO, d), lambda b, i: (b, jnp.minimum(ratio * i + ratio, nh - 1), 0)),
            pl.BlockSpec(mod.shape, const2),
            pl.BlockSpec((1, d), const2),
            pl.BlockSpec((n_groups, gc, gc), const3),
            pl.BlockSpec((n_groups, gc, gc), const3),
            pl.BlockSpec((1, d), const2),
            pl.BlockSpec((1, d), const2),
            pl.BlockSpec((n_groups, 256, 256), const3),
            pl.BlockSpec(memory_space=pl.ANY),
        ],
        out_specs=pl.BlockSpec((POOL_TILE, d), lambda b, i: (b * nt + i, 0)),
        scratch_shapes=[pltpu.VMEM((POOL_TILE + 2 * POOL_HALO, d), F32)],
        input_output_aliases={10: 0},
        compiler_params=_params("arbitrary", "arbitrary"),
        name="pool_lat",
    )(x, x, x, mod, nw2, whi, wlo, bp2, ps2, _band_matrices(256, GRID_W), jnp.zeros((t_lat + t_ctx, d), F32))
    const1 = lambda b: (0, 0)
    const13 = lambda b: (0, 0, 0)
    return pl.pallas_call(
        functools.partial(_pool_ctx_kernel, d=d, ctx_row=bsz),
        out_shape=jax.ShapeDtypeStruct((t_lat + t_ctx, d), F32),
        grid=(bsz,),
        in_specs=[
            pl.BlockSpec((None, c_len, d), lambda b: (b, 0, 0)),
            pl.BlockSpec(mod.shape, const1),
            pl.BlockSpec((1, d), const1),
            pl.BlockSpec((n_groups, gc, gc), const13),
            pl.BlockSpec((n_groups, gc, gc), const13),
            pl.BlockSpec((1, d), const1),
            pl.BlockSpec((1, d), const1),
            pl.BlockSpec((n_groups, c_len, c_len), const13),
            pl.BlockSpec(memory_space=pl.ANY),
        ],
        out_specs=pl.BlockSpec((c_len, d), lambda b: (t_lat // c_len + b, 0)),
        input_output_aliases={8: 0},
        compiler_params=_params("arbitrary"),
        name="pool_ctx",
    )(ctx, mod, nw2, whi, wlo, bp2, ps2, _band_matrices(c_len, c_len), lat)


def _for_rows(n, fn):
    def body(step, carry):
        base = pl.multiple_of(step * ROW_UNROLL, ROW_UNROLL)
        for u in range(ROW_UNROLL):
            fn(base + u)
        return carry

    lax.fori_loop(0, n // ROW_UNROLL, body, 0)


def _for_count(n, fn):
    def body(j, carry):
        fn(j)
        return carry

    lax.fori_loop(0, n, body, 0)


def _group(ref, g):
    return ref.at[pl.ds(pl.multiple_of(g * GROUP, GROUP), GROUP)]


def _mod_row(tile, lat_tiles, tiles_per_batch, ctx_row):
    return jnp.where(tile < lat_tiles, tile // tiles_per_batch, ctx_row)


def _route_kernel(x_ref, mod_ref, nw_ref, wrh_ref, wrl_ref, br_ref, ls_ref, up_ref,
                  rtok_ref, rt_ref, ng_ref, cb_ref, cnt_ref, carry_ref, *, d, lat_tiles, tiles_per_batch, ctx_row):
    i = pl.program_id(0)
    row = _mod_row(i, lat_tiles, tiles_per_batch, ctx_row)
    sh = mod_ref[pl.ds(row, 1), 3 * d:4 * d]
    sc = mod_ref[pl.ds(row, 1), 4 * d:5 * d]
    h = _norm_mod(x_ref[...], nw_ref[...], sc, sh)
    logits = _dot3(h, wrh_ref[...], wrl_ref[...]) + br_ref[...]
    lane = lax.broadcasted_iota(jnp.int32, logits.shape, 1).astype(F32)
    neg = jnp.float32(-jnp.inf)
    big = jnp.float32(1e9)
    is_g = lane < N_GROUPS
    lg = jnp.where(is_g, logits, neg)
    gmax = jnp.max(lg, axis=1, keepdims=True)
    g_idx = jnp.min(jnp.where(lg == gmax, lane, big), axis=1, keepdims=True)
    pg_top = 1.0 / jnp.sum(jnp.where(is_g, jnp.exp(lg - gmax), 0.0), axis=1, keepdims=True)
    lo = N_GROUPS + EXPERTS_PER_GROUP * g_idx
    le = jnp.where((lane >= lo) & (lane < lo + EXPERTS_PER_GROUP), logits, neg)
    e1 = jnp.max(le, axis=1, keepdims=True)
    i1 = jnp.min(jnp.where(le == e1, lane, big), axis=1, keepdims=True)
    le2 = jnp.where(lane == i1, neg, le)
    e2 = jnp.max(le2, axis=1, keepdims=True)
    i2 = jnp.min(jnp.where(le2 == e2, lane, big), axis=1, keepdims=True)
    r21 = jnp.exp(e2 - e1)
    w1 = pg_top / (1.0 + r21)
    w2 = pg_top * r21 / (1.0 + r21)
    id1 = i1 - N_GROUPS
    id2 = i2 - N_GROUPS

    @pl.when(i == 0)
    def _():
        carry_ref[...] = jnp.zeros_like(carry_ref)

    oh1 = (lane == id1).astype(F32)
    oh2 = (lane == id2).astype(F32)
    cum1 = _dot(ls_ref[...], oh1.astype(BF16))
    cum2 = _dot(ls_ref[...], oh2.astype(BF16))
    tot1 = jnp.sum(oh1, axis=0, keepdims=True)
    tot2 = jnp.sum(oh2, axis=0, keepdims=True)
    ng = jnp.floor((tot1 + tot2 + (GROUP - 1.0)) * (1.0 / GROUP))
    ng8 = jnp.broadcast_to(ng, (SUBLANES, ng.shape[1])).astype(BF16)
    lstart = GROUP * _dot(ng8, up_ref[...])[0:1]
    lpos1 = jnp.sum(oh1 * (cum1 + lstart), axis=1, keepdims=True)
    lpos2 = jnp.sum(oh2 * (cum2 + tot1 + lstart), axis=1, keepdims=True)
    carry = carry_ref[...]
    ng_ref[0] = ng
    cb_ref[0] = carry
    carry = carry + ng
    carry_ref[...] = carry
    cnt_ref[...] = carry

    fields = (id1, id2, lpos1, lpos2, w1, w2)
    slab = jnp.zeros_like(logits)
    for k, f in enumerate(fields):
        slab = jnp.where(lane == k, f, slab)
    rtok_ref[...] = slab
    rt_ref[...] = slab.T[0:SUBLANES]


def _dispatch_kernel(zl_ref, tab_ref, x_ref, mod_ref, nw_ref, *rest, d, lat_tiles, tiles_per_batch, ctx_row, n_zero_max,
                     zero_fill):
    o_ref, h_ref, l_ref, z_ref, cnt_ref, sem, zsem = rest[-7:]
    i = pl.program_id(0)
    tab_n = tab_ref.shape[-1]
    slot = i % 2

    @pl.when(i == 0)
    def _():
        l_ref[...] = jnp.zeros_like(l_ref)
        z_ref[...] = jnp.zeros_like(z_ref)

        def zcopy(z):
            return pltpu.make_async_copy(z_ref, _group(o_ref, zl_ref[z]), zsem)

        if zero_fill:
            _for_count(zl_ref[n_zero_max], lambda z: zcopy(z).start())
            _for_count(zl_ref[n_zero_max], lambda z: zcopy(z).wait())

    row = _mod_row(i, lat_tiles, tiles_per_batch, ctx_row)
    sh = mod_ref[pl.ds(row, 1), 3 * d:4 * d]
    sc = mod_ref[pl.ds(row, 1), 4 * d:5 * d]
    h_ref[...] = _norm_mod(x_ref[...], nw_ref[...], sc, sh).reshape(h_ref.shape)

    def place(r):
        v = h_ref[r]
        l_ref[slot, tab_ref[0, 0, r]] = v
        l_ref[slot, tab_ref[0, 0, TOK_TILE + r]] = v

    _for_rows(TOK_TILE, place)

    def gcopy(s, j, dst):
        return pltpu.make_async_copy(_group(l_ref.at[s], j), _group(o_ref, dst), sem.at[s])

    n_groups = tab_ref[0, 0, tab_n - 1]
    _for_count(n_groups, lambda j: gcopy(slot, j, tab_ref[0, 0, 2 * TOK_TILE + j]).start())

    @pl.when(i > 0)
    def _():
        _for_count(cnt_ref[0], lambda j: gcopy(1 - slot, j, 0).wait())

    cnt_ref[0] = n_groups

    @pl.when(i == pl.num_programs(0) - 1)
    def _():
        _for_count(n_groups, lambda j: gcopy(slot, j, 0).wait())


def _gmm_kernel(te_ref, nu_ref, x_ref, wg_ref, wu_ref, wd_ref, o_ref, wgb_ref, wub_ref, wdb_ref):
    i = pl.program_id(0)

    @pl.when((i == 0) | (te_ref[i] != te_ref[jnp.maximum(i - 1, 0)]))
    def _():
        wgb_ref[...] = wg_ref[...].astype(BF16)
        wub_ref[...] = wu_ref[...].astype(BF16)
        wdb_ref[...] = wd_ref[...].astype(BF16)

    @pl.when(i < nu_ref[0])
    def _():
        x = x_ref[...].reshape(x_ref.shape[0], wgb_ref.shape[0]).astype(BF16)
        hid = _silu(_dot(x, wgb_ref[...])) * _dot(x, wub_ref[...])
        o_ref[...] = _dot(hid.astype(BF16), wdb_ref[...]).reshape(o_ref.shape)


def _combine_kernel(tab_ref, nxt_ref, x_ref, rtok_ref, mod_ref, nf_ref, y_ref, o_ref, l_ref, g1_ref, g2_ref, sem,
                    *, d, lat_tiles, tiles_per_batch, ctx_row, final_norm):
    i = pl.program_id(0)
    tab_n = tab_ref.shape[-1]
    slot = i % 2
    row = _mod_row(i, lat_tiles, tiles_per_batch, ctx_row)
    gt = mod_ref[pl.ds(row, 1), 5 * d:6 * d]

    def gcopy(s, j, src):
        return pltpu.make_async_copy(_group(y_ref, src), _group(l_ref.at[s], j), sem.at[s])

    @pl.when(i == 0)
    def _():
        _for_count(tab_ref[0, 0, tab_n - 1], lambda j: gcopy(0, j, tab_ref[0, 0, 2 * TOK_TILE + j]).start())

    @pl.when(i < pl.num_programs(0) - 1)
    def _():
        _for_count(nxt_ref[0, 0, tab_n - 1], lambda j: gcopy(1 - slot, j, nxt_ref[0, 0, 2 * TOK_TILE + j]).start())

    _for_count(tab_ref[0, 0, tab_n - 1], lambda j: gcopy(slot, j, 0).wait())

    def pick(r):
        g1_ref[r] = l_ref[slot, tab_ref[0, 0, r]]
        g2_ref[r] = l_ref[slot, tab_ref[0, 0, TOK_TILE + r]]

    _for_rows(TOK_TILE, pick)
    rt = rtok_ref[...]
    w1 = rt[:, 4:5]
    w2 = rt[:, 5:6]
    out = x_ref[...] + gt * (w1 * g1_ref[...].reshape(x_ref.shape) + w2 * g2_ref[...].reshape(x_ref.shape))
    if final_norm:
        out = out * lax.rsqrt(jnp.mean(out * out, axis=-1, keepdims=True) + EPS) * nf_ref[...]
    o_ref[...] = out


def _moe_layer(xflat, n_tok, mod, nw, w_rg, b_rg, w_re, b_re, wg, wu, wd, layer, *, t_lat, n, ctx_row, norm_final,
               n_tok_max, sorted_prev=None):
    d = xflat.shape[1]
    n_exp = w_re.shape[1]
    tiles = n_tok // TOK_TILE
    lat_tiles = t_lat // TOK_TILE
    tiles_per_batch = n // TOK_TILE
    common = dict(d=d, lat_tiles=lat_tiles, tiles_per_batch=tiles_per_batch, ctx_row=ctx_row)
    nw2 = nw.reshape(1, d)
    wr = jnp.zeros((d, LANES), F32).at[:, :N_GROUPS].set(w_rg).at[:, N_GROUPS:N_GROUPS + n_exp].set(w_re)
    br = jnp.zeros((1, LANES), F32).at[0, :N_GROUPS].set(b_rg).at[0, N_GROUPS:N_GROUPS + n_exp].set(b_re)
    wrh, wrl = _split_bf16(wr)
    lstrict = jnp.asarray(np.tril(np.ones((TOK_TILE, TOK_TILE), np.float32), -1), dtype=BF16)
    upper = jnp.asarray(np.triu(np.ones((LANES, LANES), np.float32), 1), dtype=BF16)
    const = lambda i: (0, 0)
    per_tile = pl.BlockSpec((1, 1, LANES), lambda i: (i, 0, 0))
    rtok, rt, ngf, cbf, cnt = pl.pallas_call(
        functools.partial(_route_kernel, **common),
        out_shape=(jax.ShapeDtypeStruct((n_tok, LANES), F32),
                   jax.ShapeDtypeStruct((SUBLANES, n_tok), F32),
                   jax.ShapeDtypeStruct((tiles, 1, LANES), F32),
                   jax.ShapeDtypeStruct((tiles, 1, LANES), F32),
                   jax.ShapeDtypeStruct((1, LANES), F32)),
        grid=(tiles,),
        in_specs=[
            pl.BlockSpec((TOK_TILE, d), lambda i: (i, 0)),
            pl.BlockSpec(mod.shape, const),
            pl.BlockSpec((1, d), const),
            pl.BlockSpec((d, LANES), const),
            pl.BlockSpec((d, LANES), const),
            pl.BlockSpec((1, LANES), const),
            pl.BlockSpec((TOK_TILE, TOK_TILE), const),
            pl.BlockSpec((LANES, LANES), const),
        ],
        out_specs=(pl.BlockSpec((TOK_TILE, LANES), lambda i: (i, 0)),
                   pl.BlockSpec((SUBLANES, TOK_TILE), lambda i: (0, i)),
                   per_tile, per_tile,
                   pl.BlockSpec((1, LANES), const)),
        scratch_shapes=[pltpu.VMEM((1, LANES), F32)],
        compiler_params=_params("arbitrary"),
        name="moe_route",
    )(xflat, mod, nw2, wrh, wrl, br, lstrict, upper)

    i32 = jnp.int32
    experts = jnp.arange(n_exp, dtype=i32)
    ng_t = ngf[:, 0, :n_exp].astype(i32)
    cb_t = cbf[:, 0, :n_exp].astype(i32)
    tot_g = cnt[0, :n_exp].astype(i32)
    gpt = MOE_TILE // GROUP
    local_groups = (2 * TOK_TILE) // GROUP + n_exp
    n_sorted_tiles = pl.cdiv((2 * n_tok_max) // GROUP + n_exp * (n_tok_max // TOK_TILE), gpt) + n_exp
    n_sorted_groups = n_sorted_tiles * gpt
    padded_g = ((tot_g + gpt - 1) // gpt) * gpt
    ends_g = jnp.cumsum(padded_g)
    off_g = ends_g - padded_g
    n_used = (ends_g[-1] // gpt).astype(i32).reshape(1)
    tile_ids = jnp.arange(n_sorted_tiles, dtype=i32)
    tile_expert = jnp.sum(tile_ids[:, None] * gpt >= ends_g[None, :], axis=1).astype(i32)
    last_expert = jnp.sum(jnp.maximum(n_used - 1, 0) * gpt >= ends_g).astype(i32)
    tile_expert = jnp.minimum(tile_expert, last_expert)
    lend = jnp.cumsum(ng_t, axis=1)
    jg = jnp.arange(local_groups, dtype=i32)
    seg = jnp.sum(jg[None, :, None] >= lend[:, None, :], axis=-1).astype(i32)
    base = off_g[None, :] + cb_t - (lend - ng_t)
    dst = jnp.sum(jnp.where(seg[:, :, None] == experts, base[:, None, :], 0), axis=-1) + jg[None, :]
    dst = jnp.where(seg < n_exp, dst, 0)
    lpos = rt[2:4].astype(i32).reshape(2, tiles, TOK_TILE).transpose(1, 0, 2).reshape(tiles, 2 * TOK_TILE)
    tab_n = 2 * TOK_TILE + pl.cdiv(local_groups + 1, LANES) * LANES
    fill = jnp.zeros((tiles, tab_n - 2 * TOK_TILE - local_groups - 1), i32)
    tab = jnp.concatenate([lpos, dst.astype(i32), fill, lend[:, -1:]], axis=1).reshape(tiles, 1, tab_n)
    n_zero_max = n_sorted_groups - (2 * n_tok) // GROUP
    zc = jnp.concatenate([padded_g - tot_g, n_sorted_groups - ends_g[-1:]])
    zs = jnp.concatenate([off_g + tot_g, ends_g[-1:]])
    zend = jnp.cumsum(zc)
    zi = jnp.arange(n_zero_max, dtype=i32)
    zseg = jnp.sum(zi[:, None] >= zend[None, :], axis=-1).astype(i32)
    zl = jnp.sum(jnp.where(zseg[:, None] == jnp.arange(n_exp + 1, dtype=i32), (zs - (zend - zc))[None, :], 0), axis=-1) + zi
    zlist = jnp.concatenate([jnp.where(zi < zend[-1], zl, 0), zend[-1:]]).astype(i32)

    n_sorted = n_sorted_groups * GROUP
    row = (d // LANES, LANES)
    local_rows = local_groups * GROUP
    reuse = sorted_prev is not None
    xs_sorted = pl.pallas_call(
        functools.partial(_dispatch_kernel, n_zero_max=n_zero_max, zero_fill=not reuse, **common),
        out_shape=jax.ShapeDtypeStruct((n_sorted,) + row, F32),
        grid_spec=pltpu.PrefetchScalarGridSpec(
            num_scalar_prefetch=1,
            grid=(tiles,),
            in_specs=[
                pl.BlockSpec((1, 1, tab_n), lambda i, zl: (i, 0, 0), memory_space=pltpu.SMEM),
                pl.BlockSpec((TOK_TILE, d), lambda i, zl: (i, 0)),
                pl.BlockSpec(mod.shape, lambda i, zl: (0, 0)),
                pl.BlockSpec((1, d), lambda i, zl: (0, 0)),
            ] + ([pl.BlockSpec(memory_space=pl.ANY)] if reuse else []),
            out_specs=pl.BlockSpec(memory_space=pl.ANY),
            scratch_shapes=[pltpu.VMEM((TOK_TILE,) + row, F32), pltpu.VMEM((2, local_rows) + row, F32),
                            pltpu.VMEM((GROUP,) + row, F32), pltpu.SMEM((1,), jnp.int32),
                            pltpu.SemaphoreType.DMA((2,)), pltpu.SemaphoreType.DMA(())],
        ),
        input_output_aliases={5: 0} if reuse else {},
        compiler_params=_params("arbitrary"),
        name="moe_dispatch",
    )(zlist, tab, xflat, mod, nw2, *([sorted_prev] if reuse else []))

    de = wg.shape[-1]
    y_sorted = pl.pallas_call(
        _gmm_kernel,
        out_shape=jax.ShapeDtypeStruct((n_sorted,) + row, F32),
        grid_spec=pltpu.PrefetchScalarGridSpec(
            num_scalar_prefetch=2,
            grid=(n_sorted_tiles,),
            in_specs=[
                pl.BlockSpec((MOE_TILE,) + row, lambda i, te, nu: (jnp.minimum(i, jnp.maximum(nu[0] - 1, 0)), 0, 0)),
                pl.BlockSpec((None, None, d, de), lambda i, te, nu: (layer, te[i], 0, 0)),
                pl.BlockSpec((None, None, d, de), lambda i, te, nu: (layer, te[i], 0, 0)),
                pl.BlockSpec((None, None, de, d), lambda i, te, nu: (layer, te[i], 0, 0)),
            ],
            out_specs=pl.BlockSpec((MOE_TILE,) + row, lambda i, te, nu: (jnp.minimum(i, jnp.maximum(nu[0] - 1, 0)), 0, 0)),
            scratch_shapes=[pltpu.VMEM((d, de), BF16), pltpu.VMEM((d, de), BF16), pltpu.VMEM((de, d), BF16)],
        ),
        input_output_aliases={2: 0},
        compiler_params=_params("arbitrary"),
        name="moe_gmm",
    )(tile_expert, n_used, xs_sorted, wg, wu, wd)

    final = norm_final is not None
    nf = (norm_final if final else jnp.ones((d,), F32)).reshape(1, d)
    out = pl.pallas_call(
        functools.partial(_combine_kernel, final_norm=final, **common),
        out_shape=jax.ShapeDtypeStruct((n_tok, d), F32),
        grid=(tiles,),
        in_specs=[
            pl.BlockSpec((1, 1, tab_n), lambda i: (i, 0, 0), memory_space=pltpu.SMEM),
            pl.BlockSpec((1, 1, tab_n), lambda i: (jnp.minimum(i + 1, tiles - 1), 0, 0), memory_space=pltpu.SMEM),
            pl.BlockSpec((TOK_TILE, d), lambda i: (i, 0)),
            pl.BlockSpec((TOK_TILE, LANES), lambda i: (i, 0)),
            pl.BlockSpec(mod.shape, const),
            pl.BlockSpec((1, d), const),
            pl.BlockSpec(memory_space=pl.ANY),
        ],
        out_specs=pl.BlockSpec((TOK_TILE, d), lambda i: (i, 0)),
        scratch_shapes=[pltpu.VMEM((2, local_rows) + row, F32), pltpu.VMEM((TOK_TILE,) + row, F32),
                        pltpu.VMEM((TOK_TILE,) + row, F32), pltpu.SemaphoreType.DMA((2,))],
        compiler_params=_params("arbitrary"),
        name="moe_combine",
    )(tab, tab, xflat, rtok, mod, nf, y_sorted)
    return out, y_sorted


def _dn_proj_kernel(x_ref, xp_ref, xn_ref, mod_ref, nw_ref, wqkv_ref, wz_ref, wabh_ref, wabl_ref, wconv_ref, gpar_ref,
                    tril_ref, q_ref, k_ref, v_ref, z_ref, gct_ref, gcr_ref, hf_ref,
                    *, d, tile, tiles_per_seq, lat, ctx_row):
    i = pl.program_id(0)
    row = (i // tiles_per_seq) if lat else ctx_row
    sh = mod_ref[pl.ds(row, 1), 0:d]
    sc = mod_ref[pl.ds(row, 1), d:2 * d]
    nw = nw_ref[...]
    first = (i % tiles_per_seq) == 0
    last = (i % tiles_per_seq) == tiles_per_seq - 1
    halo = SUBLANES
    hf_ref[0:halo] = jnp.where(first, 0.0, _norm_mod(xp_ref[...], nw, sc, sh))
    hm = _norm_mod(x_ref[...], nw, sc, sh)
    hf_ref[halo:halo + tile] = hm
    hf_ref[halo + tile:] = jnp.where(last, 0.0, _norm_mod(xn_ref[...], nw, sc, sh))
    hb = hf_ref[...].astype(BF16)

    z_ref[...] = _dot(hm.astype(BF16), wz_ref[...]).astype(z_ref.dtype)

    ab = _dot3(hm, wabh_ref[...], wabl_ref[...])
    a_log, dt_bias, is_g, is_f, is_b = (gpar_ref[j:j + 1, :] for j in range(5))
    xg = ab + dt_bias
    softplus = jnp.maximum(xg, 0.0) + jnp.log(1.0 + jnp.exp(-jnp.abs(xg)))
    slab = jnp.where(is_g > 0.5, -jnp.exp(a_log) * softplus, jax.nn.sigmoid(ab))
    s_hi, s_lo = _split_bf16(slab)
    pre, tot = [], []
    for cc in range(tile // DN_CHUNK):
        rows = slice(cc * DN_CHUNK, (cc + 1) * DN_CHUNK)
        p = _dot(tril_ref[...], s_hi[rows]) + _dot(tril_ref[...], s_lo[rows])
        pre.append(p)
        tot.append(jnp.broadcast_to(p[DN_CHUNK - 1:DN_CHUNK], p.shape))
    gpre = jnp.concatenate(pre, axis=0)
    gsuf = jnp.concatenate(tot, axis=0) - gpre + slab
    gcs = jnp.where(is_f > 0.5, gpre, jnp.where(is_b > 0.5, gsuf, slab))
    gct_ref[...] = gcs
    gcs_t = gcs.T
    n_ab = 4 * DN_HEADS
    for cc in range(tile // DN_CHUNK):
        gcr_ref[cc] = gcs_t[0:n_ab, cc * DN_CHUNK:(cc + 1) * DN_CHUNK]

    dk = d // DN_HEADS
    cw = 4 * dk
    outs = (q_ref, k_ref, v_ref)
    for cc in range(3 * d // cw):
        cols = slice(cc * cw, (cc + 1) * cw)
        pc = _dot(hb, wqkv_ref[:, cols])
        wc = wconv_ref[:, cols]
        n_rows = tile + 2 * halo
        conv = (pltpu.roll(pc, 2, 0)[halo:halo + tile] * wc[0:1]
                + pltpu.roll(pc, 1, 0)[halo:halo + tile] * wc[1:2]
                + pc[halo:halo + tile] * wc[2:3]
                + pltpu.roll(pc, n_rows - 1, 0)[halo:halo + tile] * wc[3:4])
        act = _silu(conv)
        which = (cc * cw) // d
        for hh in range(cw // dk):
            head = ((cc * cw) % d) // dk + hh
            t = act[:, hh * dk:(hh + 1) * dk]
            if which < 2:
                t = t * lax.rsqrt(jnp.sum(t * t, axis=-1, keepdims=True) + EPS)
            if which == 0:
                t = t * (dk ** -0.5)
            outs[which][head] = t


def _dn_proj(xflat, tok_off, n_seq, seq_len, tile, lat, mod, nw, wqkv, wz, wab, wconv, gpar, ctx_row):
    d = xflat.shape[1]
    dk = d // DN_HEADS
    t_n = n_seq * seq_len
    tiles = t_n // tile
    tiles_per_seq = seq_len // tile
    boff = tok_off // tile
    hoff = tok_off // SUBLANES
    hper = tile // SUBLANES
    hmax = xflat.shape[0] // SUBLANES - 1
    tril = jnp.asarray(np.tril(np.ones((DN_CHUNK, DN_CHUNK), np.float32)), dtype=BF16)
    wabh, wabl = _split_bf16(wab)
    const = lambda i: (0, 0)
    hshape = jax.ShapeDtypeStruct((DN_HEADS, t_n, dk), F32)
    hspec = pl.BlockSpec((DN_HEADS, tile, dk), lambda i: (0, i, 0))
    n_ab = 4 * DN_HEADS
    return pl.pallas_call(
        functools.partial(_dn_proj_kernel, d=d, tile=tile, tiles_per_seq=tiles_per_seq, lat=lat, ctx_row=ctx_row),
        out_shape=(hshape, hshape, hshape,
                   jax.ShapeDtypeStruct((t_n, d), BF16),
                   jax.ShapeDtypeStruct((t_n, LANES), F32),
                   jax.ShapeDtypeStruct((t_n // DN_CHUNK, n_ab, DN_CHUNK), F32)),
        grid=(tiles,),
        in_specs=[
            pl.BlockSpec((tile, d), lambda i: (boff + i, 0)),
            pl.BlockSpec((SUBLANES, d), lambda i: (jnp.maximum(hoff + i * hper - 1, 0), 0)),
            pl.BlockSpec((SUBLANES, d), lambda i: (jnp.minimum(hoff + (i + 1) * hper, hmax), 0)),
            pl.BlockSpec(mod.shape, const),
            pl.BlockSpec((1, d), const),
            pl.BlockSpec(wqkv.shape, const),
            pl.BlockSpec(wz.shape, const),
            pl.BlockSpec(wab.shape, const),
            pl.BlockSpec(wab.shape, const),
            pl.BlockSpec(wconv.shape, const),
            pl.BlockSpec(gpar.shape, const),
            pl.BlockSpec((DN_CHUNK, DN_CHUNK), const),
        ],
        out_specs=(hspec, hspec, hspec,
                   pl.BlockSpec((tile, d), lambda i: (i, 0)),
                   pl.BlockSpec((tile, LANES), lambda i: (i, 0)),
                   pl.BlockSpec((tile // DN_CHUNK, n_ab, DN_CHUNK), lambda i: (i, 0, 0))),
        scratch_shapes=[pltpu.VMEM((tile + 2 * SUBLANES, d), F32)],
        compiler_params=_params("arbitrary"),
        name="dn_proj_lat" if lat else "dn_proj_ctx",
    )(xflat, xflat, xflat, mod, nw.reshape(1, d), wqkv, wz, wabh, wabl, wconv, gpar, tril)


def _dn_chunk_kernel(q_ref, k_ref, v_ref, gct_ref, gcr_ref, s0_ref, *rest, blk, reverse, need_output):
    n_out = 2 if need_output else 1
    o_ref = rest[0] if need_output else None
    sf_ref, s_ref = rest[n_out - 1], rest[n_out]
    bufs = rest[n_out + 1:]
    sets = (bufs[:len(bufs) // 2], bufs[len(bufs) // 2:])
    j = pl.program_id(1)
    nchunk = blk // DN_CHUNK
    c = DN_CHUNK
    dv = v_ref.shape[-1]
    heads = range(DN_HEADS)

    @pl.when(j == 0)
    def _():
        s_ref[...] = s0_ref[...]

    ii = lax.broadcasted_iota(jnp.int32, (c, c), 0)
    jj = lax.broadcasted_iota(jnp.int32, (c, c), 1)
    strict = (ii < jj) if reverse else (ii > jj)
    incl = (ii <= jj) if reverse else (ii >= jj)
    field = 2 if reverse else 0
    nt_dims = (((1,), (1,)), ((), ()))

    def g_last(gcc):
        return gcc[0:1, :] if reverse else gcc[c - 1:c, :]

    def prep(pair, buf):
        u_s, wq_s, kdt_s = buf[:3]
        inst = []
        for cc in range(DN_PREP_CHUNKS):
            ci = pair * DN_PREP_CHUNKS + cc
            rows = slice(ci * c, (ci + 1) * c)
            gcols = gct_ref[rows, :]
            grows = gcr_ref[ci]
            for h in heads:
                ln = 4 * h + field
                inst.append((cc, rows, h, gcols[:, ln:ln + 1], gcols[:, ln + 1:ln + 2], grows[ln:ln + 1, :]))
        kc = [k_ref[h, rows, :] for _, rows, h, _, _, _ in inst]
        vc = [v_ref[h, rows, :] for _, rows, h, _, _, _ in inst]
        decay = [jnp.exp(jnp.where(incl, gcc - gcr, 0.0)) for _, _, _, gcc, _, gcr in inst]
        egc = [jnp.exp(t[3]) for t in inst]
        kb = [kc_i * t[4] for kc_i, t in zip(kc, inst)]
        if need_output:
            qc = [q_ref[h, rows, :] for _, rows, h, _, _, _ in inst]
            lhs = [jnp.concatenate([kb_i, qc_i], axis=0).astype(BF16) for kb_i, qc_i in zip(kb, qc)]
        else:
            lhs = [kb_i.astype(BF16) for kb_i in kb]
        kq = [lax.dot_general(l_i, kc_i.astype(BF16), nt_dims, preferred_element_type=F32)
              for l_i, kc_i in zip(lhs, kc)]
        e = [jnp.where(strict, -(kq_i[0:c] * d_i), 0.0) for kq_i, d_i in zip(kq, decay)]
        pb = [e_i.astype(BF16) for e_i in e]
        p = [_dot(p_i, p_i) for p_i in pb]
        for level in range(1, 6):
            pb = [p_i.astype(BF16) for p_i in p]
            if level < 5:
                out = [_dot(jnp.concatenate([e_i.astype(BF16), pb_i], axis=0), pb_i) for pb_i, e_i in zip(pb, e)]
                e = [e_i + p_i + o_i[0:c] for e_i, p_i, o_i in zip(e, p, out)]
                p = [o_i[c:2 * c] for o_i in out]
            else:
                e = [e_i + p_i + _dot(e_i.astype(BF16), pb_i) for e_i, p_i, pb_i in zip(e, p, pb)]
        rhs = [jnp.concatenate([vc_i * t[4], kb_i * e_i], axis=1) for vc_i, kb_i, e_i, t in zip(vc, kb, egc, inst)]
        x = [r_i + _dot(e_i.astype(BF16), r_i.astype(BF16)) for e_i, r_i in zip(e, rhs)]
        for n_i, (cc, _, h, gcc, _, _) in enumerate(inst):
            loc = slice(cc * c, (cc + 1) * c)
            u_s[h, loc, :] = x[n_i][:, 0:dv]
            kdt_s[h, cc] = (kc[n_i] * jnp.exp(g_last(gcc) - gcc)).T.astype(BF16)
            if need_output:
                wq_s[h, cc] = jnp.concatenate([x[n_i][:, dv:], qc[n_i] * egc[n_i]], axis=0).astype(BF16)
                buf[3][h, loc, :] = jnp.where(incl, kq[n_i][c:2 * c] * decay[n_i], 0.0).astype(BF16)
            else:
                wq_s[h, cc] = x[n_i][:, dv:].astype(BF16)

    def recur(pair, cc, buf):
        u_s, wq_s, kdt_s = buf[:3]
        ci = pair * DN_PREP_CHUNKS + cc
        rows = slice(ci * c, (ci + 1) * c)
        loc = slice(cc * c, (cc + 1) * c)
        gcols = gct_ref[rows, :]
        s = [s_ref[h] for h in heads]
        sb = [s_h.astype(BF16) for s_h in s]
        ws = [_dot(wq_s[h, cc], sb[h]) for h in heads]
        vb = [(u_s[h, loc, :] - ws[h][0:c]).astype(BF16) for h in heads]
        if need_output:
            for h in heads:
                o_ref[h, rows, :] = (ws[h][c:2 * c] + _dot(buf[3][h, loc, :], vb[h])).astype(o_ref.dtype)
        for h in heads:
            eg = jnp.exp(g_last(gcols[:, 4 * h + field:4 * h + field + 1]))
            s_ref[h] = s[h] * eg + _dot(kdt_s[h, cc], vb[h])

    pairs = list(range(nchunk // DN_PREP_CHUNKS))
    within = list(range(DN_PREP_CHUNKS))
    if reverse:
        pairs.reverse()
        within.reverse()
    prep(pairs[0], sets[0])
    for n_p, pair in enumerate(pairs):
        for cc in within:
            recur(pair, cc, sets[n_p % 2])
        if n_p + 1 < len(pairs):
            prep(pairs[n_p + 1], sets[(n_p + 1) % 2])

    @pl.when(j == pl.num_programs(1) - 1)
    def _():
        sf_ref[...] = s_ref[...]


def _dn_scan(q, k, v, gct, gcr, s0, n_seq, seq_len, blk, reverse, need_output):
    dk = q.shape[-1]
    nblk = seq_len // blk
    order = (lambda j: nblk - 1 - j) if reverse else (lambda j: j)
    hspec = pl.BlockSpec((DN_HEADS, blk, dk), lambda b, j: (0, b * nblk + order(j), 0))
    sspec = pl.BlockSpec((None, DN_HEADS, dk, dk), lambda b, j: (b, 0, 0, 0))
    n_ab = gcr.shape[1]
    s_shape = jax.ShapeDtypeStruct((n_seq, DN_HEADS, dk, dk), F32)
    out_shape = [s_shape]
    out_specs = [sspec]
    wq_rows = 2 * DN_CHUNK if need_output else DN_CHUNK
    pair_rows = DN_PREP_CHUNKS * DN_CHUNK
    buf_set = [pltpu.VMEM((DN_HEADS, pair_rows, dk), F32),
               pltpu.VMEM((DN_HEADS, DN_PREP_CHUNKS, wq_rows, dk), BF16),
               pltpu.VMEM((DN_HEADS, DN_PREP_CHUNKS, dk, DN_CHUNK), BF16)]
    if need_output:
        out_shape.insert(0, jax.ShapeDtypeStruct(q.shape, BF16))
        out_specs.insert(0, hspec)
        buf_set.append(pltpu.VMEM((DN_HEADS, pair_rows, DN_CHUNK), BF16))
    scratch = [pltpu.VMEM((DN_HEADS, dk, dk), F32)] + buf_set + buf_set
    res = pl.pallas_call(
        functools.partial(_dn_chunk_kernel, blk=blk, reverse=reverse, need_output=need_output),
        out_shape=tuple(out_shape),
        grid=(n_seq, nblk),
        in_specs=[
            hspec, hspec, hspec,
            pl.BlockSpec((blk, LANES), lambda b, j: (b * nblk + order(j), 0)),
            pl.BlockSpec((blk // DN_CHUNK, n_ab, DN_CHUNK), lambda b, j: (b * nblk + order(j), 0, 0)),
            sspec,
        ],
        out_specs=tuple(out_specs),
        scratch_shapes=scratch,
        compiler_params=_params("arbitrary", "arbitrary"),
        name="dn_scan_" + ("bwd" if reverse else "fwd") + ("_out" if need_output else "_state"),
    )(q, k, v, gct, gcr, s0)
    return (res[0], res[1]) if need_output else (None, res[0])


def _dn_out_kernel(of_ref, ob_ref, z_ref, x_ref, mod_ref, nrm_ref, wout_ref, o_ref, *, d, tiles_per_batch):
    i = pl.program_id(0)
    row = i // tiles_per_batch
    gt = mod_ref[pl.ds(row, 1), 2 * d:3 * d]
    dk = d // DN_HEADS
    parts = []
    for h in range(DN_HEADS):
        o = of_ref[h].astype(F32) + ob_ref[h].astype(F32)
        o = o * lax.rsqrt(jnp.mean(o * o, axis=-1, keepdims=True) + EPS) * nrm_ref[...]
        parts.append((o * _silu(z_ref[:, h * dk:(h + 1) * dk].astype(F32))).astype(BF16))
    y = _dot(jnp.concatenate(parts, axis=1), wout_ref[...])
    o_ref[...] = x_ref[...] + gt * y


def _deltanet_layer(xflat, bsz, n, c_len, mod, nw, w_in, w_conv, a_log, dt_bias, dn_norm, w_out):
    d = xflat.shape[1]
    dk = d // DN_HEADS
    t_lat = bsz * n
    ctx_row = bsz
    wqkv = w_in[:, :3 * d].astype(BF16)
    wz = w_in[:, 3 * d:4 * d].astype(BF16)
    n_ab = 4 * DN_HEADS
    wab = w_in[:, 4 * d:].reshape(d, 4, DN_HEADS).transpose(0, 2, 1).reshape(d, n_ab)
    wab = jnp.zeros((d, LANES), F32).at[:, :n_ab].set(wab)
    lane = np.arange(LANES)
    fld, head, used = lane % 4, np.minimum(lane // 4, DN_HEADS - 1), lane < n_ab
    is_g = used & (fld % 2 == 0)
    dirn = fld // 2
    gpar = jnp.zeros((SUBLANES, LANES), F32)
    gpar = gpar.at[0].set(jnp.where(is_g, a_log[dirn, head], 0.0))
    gpar = gpar.at[1].set(jnp.where(is_g, dt_bias[dirn, head], 0.0))
    gpar = gpar.at[2].set(jnp.asarray(is_g, F32))
    gpar = gpar.at[3].set(jnp.asarray(used & (fld == 0), F32))
    gpar = gpar.at[4].set(jnp.asarray(used & (fld == 2), F32))

    proj = functools.partial(_dn_proj, xflat, mod=mod, nw=nw, wqkv=wqkv, wz=wz, wab=wab, wconv=w_conv, gpar=gpar,
                             ctx_row=ctx_row)
    q, k, v, z, gct, gcr = proj(tok_off=0, n_seq=bsz, seq_len=n, tile=TOK_TILE, lat=True)
    qc, kc, vc, _, gctc, gcrc = proj(tok_off=t_lat, n_seq=bsz, seq_len=c_len, tile=c_len, lat=False)

    s0 = jnp.zeros((bsz, DN_HEADS, dk, dk), F32)
    o_dir = []
    for reverse in (False, True):
        _, s_ctx = _dn_scan(qc, kc, vc, gctc, gcrc, s0, bsz, c_len, c_len, reverse, False)
        o, _ = _dn_scan(q, k, v, gct, gcr, s_ctx, bsz, n, TOK_TILE, reverse, True)
        o_dir.append(o)

    tiles = t_lat // TOK_TILE
    const = lambda i: (0, 0)
    hspec = pl.BlockSpec((DN_HEADS, TOK_TILE, dk), lambda i: (0, i, 0))
    return pl.pallas_call(
        functools.partial(_dn_out_kernel, d=d, tiles_per_batch=n // TOK_TILE),
        out_shape=jax.ShapeDtypeStruct((t_lat, d), F32),
        grid=(tiles,),
        in_specs=[
            hspec, hspec,
            pl.BlockSpec((TOK_TILE, d), lambda i: (i, 0)),
            pl.BlockSpec((TOK_TILE, d), lambda i: (i, 0)),
            pl.BlockSpec(mod.shape, const),
            pl.BlockSpec((1, dk), const),
            pl.BlockSpec((d, d), const),
        ],
        out_specs=pl.BlockSpec((TOK_TILE, d), lambda i: (i, 0)),
        compiler_params=_params("arbitrary"),
        name="dn_out",
    )(o_dir[0], o_dir[1], z, xflat, mod, dn_norm.reshape(1, dk), w_out.astype(BF16))


def kernel(x, c, ctx, c_ctx, w_ada, b_ada, norm_mix, norm_ffn, w_pool, b_pool, pool_scale, w_dn_in, w_dn_conv, dn_a_log, dn_dt_bias, dn_norm, w_dn_out, w_rg, b_rg, w_re, b_re, w_e_gate, w_e_up, w_e_down, norm_final):
    bsz, n, d = x.shape
    c_len = ctx.shape[1]
    assert w_ada.shape[0] == 2 and bsz < COND_ROWS and n % POOL_TILE == 0 and c_len == 256
    assert (bsz * c_len) % TOK_TILE == 0 and d % (4 * LANES) == 0
    t_lat = bsz * n
    t_all = t_lat + bsz * c_len
    ctx_row = bsz
    cond = jnp.zeros((COND_ROWS, d), F32).at[:bsz].set(c.astype(F32)).at[ctx_row].set(c_ctx.astype(F32))
    mod = _ada(cond, w_ada, b_ada)

    xflat = _pool_layer(x.astype(F32), ctx.astype(F32), mod[0], norm_mix[0], w_pool[0], b_pool[0], pool_scale[0])
    xflat, sorted0 = _moe_layer(xflat, t_all, mod[0], norm_ffn[0], w_rg[0], b_rg[0], w_re[0], b_re[0],
                                w_e_gate, w_e_up, w_e_down, 0, t_lat=t_lat, n=n, ctx_row=ctx_row, norm_final=None,
                                n_tok_max=t_all)

    xlat = _deltanet_layer(xflat, bsz, n, c_len, mod[1], norm_mix[1], w_dn_in[0], w_dn_conv[0], dn_a_log[0],
                           dn_dt_bias[0], dn_norm[0], w_dn_out[0])
    out, _ = _moe_layer(xlat, t_lat, mod[1], norm_ffn[1], w_rg[1], b_rg[1], w_re[1], b_re[1],
                        w_e_gate, w_e_up, w_e_down, 1, t_lat=t_lat, n=n, ctx_row=ctx_row, norm_final=norm_final,
                        n_tok_max=t_all, sorted_prev=sorted0)
    return out.reshape(bsz, n, d).astype(x.dtype)
```
